```python
import jax, jax.numpy as jnp
from jax import lax
import numpy as np

D_MODEL = 2048
BATCH = 2
SEQ = 8192
DEPTH = 4

CHUNK = 64
RET_HEADS = 8
RET_DK = 128
RET_DV = 256
RET_QK = RET_HEADS * RET_DK
RET_V = RET_HEADS * RET_DV
SGU_GROUPS = 8
SGU_LEN = 128
SGU_WIDTH = D_MODEL
SGU_CH = SGU_WIDTH // SGU_GROUPS
D_FF = -(-(8 * D_MODEL) // (3 * 256)) * 256
ROPE_BASE = 10000.0
EPS = 1e-6
SPLITS = [RET_QK, 2 * RET_QK, 2 * RET_QK + RET_V, 2 * RET_QK + 2 * RET_V,
          2 * RET_QK + 2 * RET_V + SGU_WIDTH, 2 * RET_QK + 2 * RET_V + 2 * SGU_WIDTH,
          2 * RET_QK + 2 * RET_V + 2 * SGU_WIDTH + D_MODEL]
IN_COLS = 2 * RET_QK + 2 * RET_V + 2 * SGU_WIDTH + 2 * D_MODEL

kernel_name = "hybrid_retention_sgu_griffin_merge"


def rms_norm(x, w):
    xf = x.astype(jnp.float32)
    y = xf * lax.rsqrt(jnp.mean(xf * xf, axis=-1, keepdims=True) + EPS)
    return (y * w.astype(jnp.float32)).astype(x.dtype)


def layer_norm(x, w, b):
    xf = x.astype(jnp.float32)
    mu = jnp.mean(xf, axis=-1, keepdims=True)
    var = jnp.mean(jnp.square(xf - mu), axis=-1, keepdims=True)
    y = (xf - mu) * lax.rsqrt(var + EPS) * w.astype(jnp.float32) + b.astype(jnp.float32)
    return y.astype(x.dtype)


def head_group_norm(o, w):
    B, S, H, dv = o.shape
    of = o.astype(jnp.float32)
    mu = jnp.mean(of, axis=-1, keepdims=True)
    var = jnp.mean(jnp.square(of - mu), axis=-1, keepdims=True)
    y = ((of - mu) * lax.rsqrt(var + EPS)).reshape(B, S, H * dv) * w.astype(jnp.float32)
    return y.astype(o.dtype)


def rotary(x, pos):
    half = x.shape[-1] // 2
    inv = ROPE_BASE ** (-jnp.arange(half, dtype=jnp.float32) / half)
    ang = pos.astype(jnp.float32)[:, None] * inv[None, :]
    cos = jnp.cos(ang)[None, :, None, :]
    sin = jnp.sin(ang)[None, :, None, :]
    xf = x.astype(jnp.float32)
    x1, x2 = xf[..., :half], xf[..., half:]
    return jnp.concatenate([x1 * cos - x2 * sin, x2 * cos + x1 * sin], axis=-1).astype(x.dtype)


def retention(q, k, v):
    B, S, H, dk = q.shape
    dv = v.shape[-1]
    nc = S // CHUNK
    dt = q.dtype
    log_g = jnp.log1p(-(2.0 ** (-5.0 - jnp.arange(H, dtype=jnp.float32))))
    idx = jnp.arange(CHUNK, dtype=jnp.float32)
    intra_decay = jnp.exp(log_g[:, None, None] * jnp.abs(idx[:, None] - idx[None, :]))
    q_decay = jnp.exp(log_g[None, :] * (idx[:, None] + 1.0))
    k_decay = jnp.exp(log_g[None, :] * (CHUNK - 1.0 - idx[:, None]))
    chunk_decay = jnp.exp(log_g * CHUNK)

    q = q.reshape(B, nc, CHUNK, H, dk) * (dk ** -0.5)
    k = k.reshape(B, nc, CHUNK, H, dk)
    v = v.reshape(B, nc, CHUNK, H, dv)

    scores = jnp.einsum('bnihd,bnjhd->bnhij', q, k) * intra_decay.astype(dt)
    o_intra = jnp.einsum('bnhij,bnjhe->bnihe', scores, v)

    qs = q * q_decay.astype(dt)[:, :, None]
    ks = k * k_decay.astype(dt)[:, :, None]
    cdec = chunk_decay.astype(dt)[None, :, None, None]

    def step(state, xs):
        qc, kc, vc = xs
        o = jnp.einsum('bihd,bhde->bihe', qc, state)
        state = state * cdec + jnp.einsum('bjhd,bjhe->bhde', kc, vc)
        return state, o

    s0 = jnp.zeros((B, H, dk, dv), dtype=v.dtype)
    _, o_inter = lax.scan(step, s0, (jnp.swapaxes(qs, 0, 1), jnp.swapaxes(ks, 0, 1),
                                     jnp.swapaxes(v, 0, 1)))
    o = o_intra + jnp.swapaxes(o_inter, 0, 1)
    return o.reshape(B, S, H, dv)


def spatial_gating(u, v, ln_w, ln_b, w_s, b_s):
    B, S, W = v.shape
    ng = S // SGU_LEN
    vn = layer_norm(v, ln_w, ln_b).reshape(B, ng, SGU_LEN, SGU_GROUPS, SGU_CH)
    pos = jnp.arange(SGU_LEN)
    mask = (pos[None, :] // CHUNK) <= (pos[:, None] // CHUNK)
    w = jnp.where(mask[None], w_s, jnp.zeros_like(w_s))
    mixed = jnp.einsum('gij,bnjgc->bnigc', w, vn) + b_s.T[None, None, :, :, None]
    return u * mixed.reshape(B, S, W)


def hybrid_layer(x, pos, norm_mix_w, w_in, ret_gn_w, ret_proj, sgu_ln_w, sgu_ln_b,
                 sgu_w_s, sgu_b_s, sgu_proj, w_out, norm_ffn_w, w_ffn_in, w_ffn_out):
    B, S, _ = x.shape
    h = rms_norm(x, norm_mix_w)
    z = h @ w_in
    q, k, v, g, su, sv, gate_a, gate_b = jnp.split(z, SPLITS, axis=-1)

    q = rotary(q.reshape(B, S, RET_HEADS, RET_DK), pos)
    k = rotary(k.reshape(B, S, RET_HEADS, RET_DK), pos)
    v = v.reshape(B, S, RET_HEADS, RET_DV)
    ret = head_group_norm(retention(q, k, v), ret_gn_w)
    branch_a = (jax.nn.silu(g) * ret) @ ret_proj

    zu = jax.nn.gelu(su, approximate=False)
    zv = jax.nn.gelu(sv, approximate=False)
    branch_b = spatial_gating(zu, zv, sgu_ln_w, sgu_ln_b, sgu_w_s, sgu_b_s) @ sgu_proj

    merged = jax.nn.sigmoid(gate_a) * branch_a + jax.nn.sigmoid(gate_b) * branch_b
    x = x + merged @ w_out

    h = rms_norm(x, norm_ffn_w)
    a, c = jnp.split(h @ w_ffn_in, 2, axis=-1)
    x = x + (jax.nn.silu(a) * c) @ w_ffn_out
    return x


def setup_inputs(seed: int = 0) -> dict:
    key = jax.random.key(seed)
    ks = jax.random.split(key, 16)
    f32 = jnp.float32
    nrm = lambda k, shape, scale: jax.random.normal(k, shape, f32) * scale
    return {
        "x": jax.random.normal(ks[0], (BATCH, SEQ, D_MODEL), f32),
        "norm_mix_w": 1.0 + nrm(ks[1], (DEPTH, D_MODEL), 0.02),
        "w_in": nrm(ks[2], (DEPTH, D_MODEL, IN_COLS), D_MODEL ** -0.5),
        "ret_gn_w": 1.0 + nrm(ks[3], (DEPTH, RET_V), 0.02),
        "ret_proj": nrm(ks[4], (DEPTH, RET_V, D_MODEL), RET_V ** -0.5),
        "sgu_ln_w": 1.0 + nrm(ks[5], (DEPTH, SGU_WIDTH), 0.02),
        "sgu_ln_b": nrm(ks[6], (DEPTH, SGU_WIDTH), 0.02),
        "sgu_w_s": nrm(ks[7], (DEPTH, SGU_GROUPS, SGU_LEN, SGU_LEN), SGU_LEN ** -0.5),
        "sgu_b_s": 1.0 + nrm(ks[8], (DEPTH, SGU_GROUPS, SGU_LEN), 0.02),
        "sgu_proj": nrm(ks[9], (DEPTH, SGU_WIDTH, D_MODEL), SGU_WIDTH ** -0.5),
        "w_out": nrm(ks[10], (DEPTH, D_MODEL, D_MODEL), D_MODEL ** -0.5),
        "norm_ffn_w": 1.0 + nrm(ks[11], (DEPTH, D_MODEL), 0.02),
        "w_ffn_in": nrm(ks[12], (DEPTH, D_MODEL, 2 * D_FF), D_MODEL ** -0.5),
        "w_ffn_out": nrm(ks[13], (DEPTH, D_FF, D_MODEL), D_FF ** -0.5),
        "final_norm_w": 1.0 + nrm(ks[14], (D_MODEL,), 0.02),
    }


def reference(x, norm_mix_w, w_in, ret_gn_w, ret_proj, sgu_ln_w, sgu_ln_b, sgu_w_s, sgu_b_s,
              sgu_proj, w_out, norm_ffn_w, w_ffn_in, w_ffn_out, final_norm_w):
    pos = jnp.arange(x.shape[1], dtype=jnp.int32)
    for l in range(DEPTH):
        x = hybrid_layer(x, pos, norm_mix_w[l], w_in[l], ret_gn_w[l], ret_proj[l], sgu_ln_w[l],
                         sgu_ln_b[l], sgu_w_s[l], sgu_b_s[l], sgu_proj[l], w_out[l],
                         norm_ffn_w[l], w_ffn_in[l], w_ffn_out[l])
    return rms_norm(x, final_norm_w)
```

```python
import functools

import jax
import jax.numpy as jnp
import numpy as np
from jax import lax
from jax.experimental import pallas as pl
from jax.experimental.pallas import tpu as pltpu

D_MODEL = 2048
DEPTH = 4
CHUNK = 64
RET_HEADS = 8
RET_DK = 128
RET_DV = 256
RET_QK = RET_HEADS * RET_DK
RET_V = RET_HEADS * RET_DV
SGU_GROUPS = 8
SGU_LEN = 128
SGU_CH = D_MODEL // SGU_GROUPS
D_FF = 5632
IN_COLS = 2 * RET_QK + 2 * RET_V + 2 * D_MODEL + 2 * D_MODEL
ROPE_BASE = 10000.0
EPS = 1e-6

F32 = jnp.float32
BF16 = jnp.bfloat16

V7X_VMEM_BYTES = 64 * 2**20
VMEM_LIMIT_BYTES = V7X_VMEM_BYTES - 8 * 2**20

IN_BM, IN_BN = 1024, 1024
FFN_BM, FFN_BN = 1024, 512
ROW_BM = 256
RET_T = 256
SGU_BM = 512
CAST_BM = 512

_SEG_Q, _SEG_K, _SEG_V, _SEG_G, _SEG_SU, _SEG_SV, _SEG_GATE = 0, 1, 2, 4, 6, 10, 10


def _params(*sem):
    return pltpu.CompilerParams(dimension_semantics=sem, vmem_limit_bytes=VMEM_LIMIT_BYTES)


def _sigmoid(x):
    return 1.0 / (1.0 + jnp.exp(-x))


def _silu(x):
    return x * _sigmoid(x)


def _gelu(x):
    return 0.5 * x * (1.0 + lax.erf(x * np.float32(1.0 / np.sqrt(2.0))))


def _rms_norm_rows(x, w):
    return x * lax.rsqrt(jnp.mean(x * x, axis=-1, keepdims=True) + EPS) * w


def _cast_kernel(x_ref, o_ref):
    o_ref[...] = x_ref[...].astype(o_ref.dtype)


def _cast_bf16(w):
    l, r, c = w.shape
    w2 = w.reshape(l * r, c)
    return pl.pallas_call(
        _cast_kernel,
        grid=(l * r // CAST_BM,),
        in_specs=[pl.BlockSpec((CAST_BM, c), lambda i: (i, 0))],
        out_specs=pl.BlockSpec((CAST_BM, c), lambda i: (i, 0)),
        out_shape=jax.ShapeDtypeStruct((l * r, c), BF16),
        compiler_params=_params("parallel"),
        name="cast_bf16",
    )(w2)


def _norm_kernel(x_ref, w_ref, o_ref):
    o_ref[...] = _rms_norm_rows(x_ref[...], w_ref[...]).astype(o_ref.dtype)


def _rms_norm(x, w, out_dtype):
    n, d = x.shape
    return pl.pallas_call(
        _norm_kernel,
        grid=(n // ROW_BM,),
        in_specs=[pl.BlockSpec((ROW_BM, d), lambda i: (i, 0)),
                  pl.BlockSpec((1, d), lambda i: (0, 0))],
        out_specs=pl.BlockSpec((ROW_BM, d), lambda i: (i, 0)),
        out_shape=jax.ShapeDtypeStruct((n, d), out_dtype),
        compiler_params=_params("parallel"),
        name="rms_norm",
    )(x, w.reshape(1, d))


def _in_proj_kernel(h_ref, w_ref, cos_ref, sin_ref, z_ref, wbf_ref):
    j = pl.program_id(0)

    @pl.when(pl.program_id(1) == 0)
    def _():
        wbf_ref[...] = w_ref[...].astype(BF16)

    def tile(c):
        return jnp.dot(h_ref[...], wbf_ref[:, c * 256:(c + 1) * 256], preferred_element_type=F32)

    def emit(fn):
        for c in range(IN_BN // 256):
            z_ref[:, c * 256:(c + 1) * 256] = fn(tile(c)).astype(z_ref.dtype)

    def rotary(scale):
        def fn(a):
            cos, sin = cos_ref[...], sin_ref[...]
            halves = []
            for hh in range(256 // RET_DK):
                t = a[:, hh * RET_DK:(hh + 1) * RET_DK]
                r = t * cos + pltpu.roll(t, RET_DK // 2, axis=1) * sin
                halves.append(r * scale if scale != 1.0 else r)
            return jnp.concatenate(halves, axis=1)
        return fn

    @pl.when(j == _SEG_Q)
    def _():
        emit(rotary(np.float32(RET_DK ** -0.5)))

    @pl.when(j == _SEG_K)
    def _():
        emit(rotary(1.0))

    @pl.when((j >= _SEG_V) & (j < _SEG_G))
    def _():
        emit(lambda a: a)

    @pl.when((j >= _SEG_G) & (j < _SEG_SU))
    def _():
        emit(_silu)

    @pl.when((j >= _SEG_SU) & (j < _SEG_GATE))
    def _():
        emit(_gelu)

    @pl.when(j >= _SEG_GATE)
    def _():
        emit(_sigmoid)


def _in_proj(h, w_in, layer, cos2, sin2, seq):
    n = h.shape[0]
    pos_blocks = seq // IN_BM
    return pl.pallas_call(
        _in_proj_kernel,
        grid=(IN_COLS // IN_BN, n // IN_BM),
        in_specs=[
            pl.BlockSpec((IN_BM, D_MODEL), lambda j, m: (m, 0)),
            pl.BlockSpec((None, D_MODEL, IN_BN), lambda j, m: (layer, 0, j)),
            pl.BlockSpec((IN_BM, RET_DK), lambda j, m: (m % pos_blocks, 0)),
            pl.BlockSpec((IN_BM, RET_DK), lambda j, m: (m % pos_blocks, 0)),
        ],
        out_specs=pl.BlockSpec((IN_BM, IN_BN), lambda j, m: (m, j)),
        out_shape=jax.ShapeDtypeStruct((n, IN_COLS), BF16),
        scratch_shapes=[pltpu.VMEM((D_MODEL, IN_BN), BF16)],
        compiler_params=_params("arbitrary", "arbitrary"),
        name="in_proj",
    )(h, w_in, cos2, sin2)


def _retention_kernel(q_ref, k_ref, v_ref, g_ref, dmask_ref, qdec_ref, kdec_ref, cdec_ref,
                      gnw_ref, o_ref, state_ref):
    state_ref[...] = jnp.zeros_like(state_ref)
    n_steps = q_ref.shape[0] // RET_T

    def step(t, carry):
        rows = pl.ds(pl.multiple_of(t * RET_T, RET_T), RET_T)
        q = q_ref[rows, :]
        k = k_ref[rows, :]
        v = v_ref[rows, :]
        s = lax.dot_general(q, k, (((1,), (1,)), ((), ())), preferred_element_type=F32)
        s = s * dmask_ref[...]
        o = jnp.dot(s.astype(BF16), v, preferred_element_type=F32)
        st = state_ref[...]
        o = o + qdec_ref[...] * jnp.dot(q, st.astype(BF16), preferred_element_type=F32)
        kd = (k.astype(F32) * kdec_ref[...]).astype(BF16)
        upd = lax.dot_general(kd, v, (((0,), (0,)), ((), ())), preferred_element_type=F32)
        state_ref[...] = st * cdec_ref[...] + upd
        mu = jnp.mean(o, axis=-1, keepdims=True)
        d = o - mu
        var = jnp.mean(d * d, axis=-1, keepdims=True)
        y = d * lax.rsqrt(var + EPS) * gnw_ref[...] * g_ref[rows, :].astype(F32)
        o_ref[rows, :] = y.astype(o_ref.dtype)
        return carry

    lax.fori_loop(0, n_steps, step, 0)


def _retention(z, gn_w, consts, batch, seq):
    dmask, qdec, kdec, cdec = consts
    n = z.shape[0]
    v_off = 2 * RET_QK // RET_DV
    g_off = v_off + RET_HEADS
    head_const = lambda r, c: pl.BlockSpec((None, r, c), lambda b, h: (h, 0, 0))
    return pl.pallas_call(
        _retention_kernel,
        grid=(batch, RET_HEADS),
        in_specs=[
            pl.BlockSpec((seq, RET_DK), lambda b, h: (b, h)),
            pl.BlockSpec((seq, RET_DK), lambda b, h: (b, RET_HEADS + h)),
            pl.BlockSpec((seq, RET_DV), lambda b, h: (b, v_off + h)),
            pl.BlockSpec((seq, RET_DV), lambda b, h: (b, g_off + h)),
            head_const(RET_T, RET_T),
            head_const(RET_T, RET_DV),
            head_const(RET_T, RET_DK),
            head_const(RET_DK, RET_DV),
            pl.BlockSpec((1, RET_DV), lambda b, h: (0, h)),
        ],
        out_specs=pl.BlockSpec((seq, RET_DV), lambda b, h: (b, h)),
        out_shape=jax.ShapeDtypeStruct((n, RET_V), BF16),
        scratch_shapes=[pltpu.VMEM((RET_DK, RET_DV), F32)],
        compiler_params=_params("parallel", "parallel"),
        name="retention",
    )(z, z, z, z, dmask, qdec, kdec, cdec, gn_w.reshape(1, RET_V))


def _retention_consts():
    log_g = jnp.log1p(-(2.0 ** (-5.0 - jnp.arange(RET_HEADS, dtype=F32))))
    idx = jnp.arange(RET_T, dtype=F32)
    dist = jnp.abs(idx[:, None] - idx[None, :])
    chunk = jnp.arange(RET_T) // CHUNK
    visible = chunk[None, :] <= chunk[:, None]
    dmask = jnp.where(visible[None], jnp.exp(log_g[:, None, None] * dist), 0.0)
    qdec = jnp.exp(log_g[:, None] * (idx[None, :] + 1.0))
    kdec = jnp.exp(log_g[:, None] * (RET_T - 1.0 - idx[None, :]))
    cdec = jnp.exp(log_g * RET_T)
    qdec = jnp.broadcast_to(qdec[:, :, None], (RET_HEADS, RET_T, RET_DV))
    kdec = jnp.broadcast_to(kdec[:, :, None], (RET_HEADS, RET_T, RET_DK))
    cdec = jnp.broadcast_to(cdec[:, None, None], (RET_HEADS, RET_DK, RET_DV))
    return dmask.astype(F32), qdec.astype(F32), kdec.astype(F32), cdec.astype(F32)


def _sgu_kernel(zu_ref, zv_ref, lnw_ref, lnb_ref, ws_ref, bs_ref, o_ref):
    zv = zv_ref[...].astype(F32)
    mu = jnp.mean(zv, axis=-1, keepdims=True)
    d = zv - mu
    var = jnp.mean(d * d, axis=-1, keepdims=True)
    vn = (d * lax.rsqrt(var + EPS) * lnw_ref[...] + lnb_ref[...]).astype(BF16)
    row_chunk = lax.broadcasted_iota(jnp.int32, (SGU_LEN, SGU_LEN), 0) // CHUNK
    col_chunk = lax.broadcasted_iota(jnp.int32, (SGU_LEN, SGU_LEN), 1) // CHUNK
    visible = col_chunk <= row_chunk
    for g in range(SGU_GROUPS):
        wm = jnp.where(visible, ws_ref[g], 0.0).astype(BF16)
        cols = slice(g * SGU_CH, (g + 1) * SGU_CH)
        for r in range(SGU_BM // SGU_LEN):
            rows = slice(r * SGU_LEN, (r + 1) * SGU_LEN)
            mixed = jnp.dot(wm, vn[rows, cols], preferred_element_type=F32) + bs_ref[g]
            o_ref[rows, cols] = (zu_ref[rows, cols].astype(F32) * mixed).astype(o_ref.dtype)


def _sgu(z, ln_w, ln_b, w_s, b_s):
    n = z.shape[0]
    su_off = (2 * RET_QK + 2 * RET_V) // D_MODEL
    bs_b = jnp.broadcast_to(b_s[:, :, None], (SGU_GROUPS, SGU_LEN, SGU_CH))
    return pl.pallas_call(
        _sgu_kernel,
        grid=(n // SGU_BM,),
        in_specs=[
            pl.BlockSpec((SGU_BM, D_MODEL), lambda i: (i, su_off)),
            pl.BlockSpec((SGU_BM, D_MODEL), lambda i: (i, su_off + 1)),
            pl.BlockSpec((1, D_MODEL), lambda i: (0, 0)),
            pl.BlockSpec((1, D_MODEL), lambda i: (0, 0)),
            pl.BlockSpec((SGU_GROUPS, SGU_LEN, SGU_LEN), lambda i: (0, 0, 0)),
            pl.BlockSpec((SGU_GROUPS, SGU_LEN, SGU_CH), lambda i: (0, 0, 0)),
        ],
        out_specs=pl.BlockSpec((SGU_BM, D_MODEL), lambda i: (i, 0)),
        out_shape=jax.ShapeDtypeStruct((n, D_MODEL), BF16),
        compiler_params=_params("parallel"),
        name="spatial_gating",
    )(z, z, ln_w.reshape(1, D_MODEL), ln_b.reshape(1, D_MODEL), w_s, bs_b)


def _merge_kernel(a_ref, b_ref, ga_ref, gb_ref, x_ref, wa_ref, wb_ref, wo_ref, nw_ref,
                  xo_ref, ho_ref):
    pa = jnp.dot(a_ref[...], wa_ref[...], preferred_element_type=F32)
    pb = jnp.dot(b_ref[...], wb_ref[...], preferred_element_type=F32)
    merged = ga_ref[...].astype(F32) * pa + gb_ref[...].astype(F32) * pb
    x = x_ref[...] + jnp.dot(merged.astype(BF16), wo_ref[...], preferred_element_type=F32)
    xo_ref[...] = x
    ho_ref[...] = _rms_norm_rows(x, nw_ref[...]).astype(ho_ref.dtype)


def _resident(shape, index_map):
    return pl.BlockSpec(shape, index_map, pipeline_mode=pl.Buffered(1))


def _merge(a, b, z, x, wa, wb, wo, layer, norm_w):
    n = x.shape[0]
    gate_off = (2 * RET_QK + 2 * RET_V + 2 * D_MODEL) // D_MODEL
    row = lambda c: pl.BlockSpec((ROW_BM, D_MODEL), lambda i: (i, c))
    weight = lambda: _resident((D_MODEL, D_MODEL), lambda i: (layer, 0))
    return pl.pallas_call(
        _merge_kernel,
        grid=(n // ROW_BM,),
        in_specs=[row(0), row(0), row(gate_off), row(gate_off + 1), row(0),
                  weight(), weight(), weight(),
                  pl.BlockSpec((1, D_MODEL), lambda i: (0, 0))],
        out_specs=[row(0), row(0)],
        out_shape=[jax.ShapeDtypeStruct((n, D_MODEL), F32),
                   jax.ShapeDtypeStruct((n, D_MODEL), BF16)],
        compiler_params=_params("parallel"),
        name="merge_out_proj",
    )(a, b, z, z, x, wa, wb, wo, norm_w.reshape(1, D_MODEL))


def _ffn_in_kernel(h_ref, wa_ref, wc_ref, u_ref, wabf_ref, wcbf_ref):
    @pl.when(pl.program_id(1) == 0)
    def _():
        wabf_ref[...] = wa_ref[...].astype(BF16)
        wcbf_ref[...] = wc_ref[...].astype(BF16)

    for c in range(FFN_BN // 256):
        cols = slice(c * 256, (c + 1) * 256)
        a = jnp.dot(h_ref[...], wabf_ref[:, cols], preferred_element_type=F32)
        g = jnp.dot(h_ref[...], wcbf_ref[:, cols], preferred_element_type=F32)
        u_ref[:, cols] = (_silu(a) * g).astype(u_ref.dtype)


def _ffn_in(h, w_ffn_in, layer):
    n = h.shape[0]
    nb = D_FF // FFN_BN
    return pl.pallas_call(
        _ffn_in_kernel,
        grid=(nb, n // FFN_BM),
        in_specs=[
            pl.BlockSpec((FFN_BM, D_MODEL), lambda j, m: (m, 0)),
            pl.BlockSpec((None, D_MODEL, FFN_BN), lambda j, m: (layer, 0, j)),
            pl.BlockSpec((None, D_MODEL, FFN_BN), lambda j, m: (layer, 0, nb + j)),
        ],
        out_specs=pl.BlockSpec((FFN_BM, FFN_BN), lambda j, m: (m, j)),
        out_shape=jax.ShapeDtypeStruct((n, D_FF), BF16),
        scratch_shapes=[pltpu.VMEM((D_MODEL, FFN_BN), BF16),
                        pltpu.VMEM((D_MODEL, FFN_BN), BF16)],
        compiler_params=_params("arbitrary", "arbitrary"),
        name="ffn_in",
    )(h, w_ffn_in, w_ffn_in)


def _ffn_out_kernel(u_ref, x_ref, w_ref, nw_ref, xo_ref, ho_ref):
    x = x_ref[...] + jnp.dot(u_ref[...], w_ref[...], preferred_element_type=F32)
    xo_ref[...] = x
    ho_ref[...] = _rms_norm_rows(x, nw_ref[...]).astype(ho_ref.dtype)


def _ffn_out(u, x, w, layer, norm_w, h_dtype):
    n = x.shape[0]
    row = lambda width: pl.BlockSpec((ROW_BM, width), lambda i: (i, 0))
    return pl.pallas_call(
        _ffn_out_kernel,
        grid=(n // ROW_BM,),
        in_specs=[row(D_FF), row(D_MODEL),
                  _resident((D_FF, D_MODEL), lambda i: (layer, 0)),
                  pl.BlockSpec((1, D_MODEL), lambda i: (0, 0))],
        out_specs=[row(D_MODEL), row(D_MODEL)],
        out_shape=[jax.ShapeDtypeStruct((n, D_MODEL), F32),
                   jax.ShapeDtypeStruct((n, D_MODEL), h_dtype)],
        compiler_params=_params("parallel"),
        name="ffn_out",
    )(u, x, w, norm_w.reshape(1, D_MODEL))


def _rotary_tables(seq):
    half = RET_DK // 2
    inv = ROPE_BASE ** (-jnp.arange(half, dtype=F32) / half)
    ang = jnp.arange(seq, dtype=jnp.int32).astype(F32)[:, None] * inv[None, :]
    cos, sin = jnp.cos(ang), jnp.sin(ang)
    return jnp.concatenate([cos, cos], axis=1), jnp.concatenate([-sin, sin], axis=1)


def kernel(x, norm_mix_w, w_in, ret_gn_w, ret_proj, sgu_ln_w, sgu_ln_b, sgu_w_s, sgu_b_s,
           sgu_proj, w_out, norm_ffn_w, w_ffn_in, w_ffn_out, final_norm_w):
    batch, seq, d = x.shape
    assert d == D_MODEL and seq % IN_BM == 0 and seq % RET_T == 0
    n = batch * seq
    cos2, sin2 = _rotary_tables(seq)
    ret_consts = _retention_consts()
    ret_proj_bf = _cast_bf16(ret_proj)
    sgu_proj_bf = _cast_bf16(sgu_proj)
    w_out_bf = _cast_bf16(w_out)
    w_ffn_out_bf = _cast_bf16(w_ffn_out)

    xf = x.reshape(n, d)
    h = _rms_norm(xf, norm_mix_w[0], BF16)
    for l in range(DEPTH):
        z = _in_proj(h, w_in, l, cos2, sin2, seq)
        ret = _retention(z, ret_gn_w[l], ret_consts, batch, seq)
        sgu = _sgu(z, sgu_ln_w[l], sgu_ln_b[l], sgu_w_s[l], sgu_b_s[l])
        xf, h = _merge(ret, sgu, z, xf, ret_proj_bf, sgu_proj_bf, w_out_bf, l, norm_ffn_w[l])
        u = _ffn_in(h, w_ffn_in, l)
        last = l == DEPTH - 1
        next_w = final_norm_w if last else norm_mix_w[l + 1]
        xf, h = _ffn_out(u, xf, w_ffn_out_bf, l, next_w, F32 if last else BF16)
    return h.reshape(batch, seq, d)
```

```python
import functools

import jax
import jax.numpy as jnp
import numpy as np
from jax import lax
from jax.experimental import pallas as pl
from jax.experimental.pallas import tpu as pltpu

D_MODEL = 2048
DEPTH = 4
CHUNK = 64
RET_HEADS = 8
RET_DK = 128
RET_DV = 256
RET_QK = RET_HEADS * RET_DK
RET_V = RET_HEADS * RET_DV
SGU_GROUPS = 8
SGU_LEN = 128
SGU_CH = D_MODEL // SGU_GROUPS
D_FF = 5632
IN_COLS = 2 * RET_QK + 2 * RET_V + 2 * D_MODEL + 2 * D_MODEL
ROPE_BASE = 10000.0
EPS = 1e-6

F32 = jnp.float32
BF16 = jnp.bfloat16

V7X_VMEM_BYTES = 64 * 2**20
VMEM_LIMIT_BYTES = V7X_VMEM_BYTES - 8 * 2**20

IN_BM, IN_BN = 2048, 1024
FFN_BM, FFN_BN = 2048, 512
MXU_TILE = 256
ROW_BM = 256
RET_T = 256
RET_TS = 1024
RET_HB = 4
SGU_BM = 512
CAST_BM = 512

_SEG_Q = 0
_SEG_K = RET_QK // IN_BN
_SEG_V = 2 * RET_QK // IN_BN
_SEG_G = _SEG_V + RET_V // IN_BN
_SEG_SU = _SEG_G + RET_V // IN_BN
_SEG_GATE = _SEG_SU + 2 * D_MODEL // IN_BN


def _params(*sem):
    return pltpu.CompilerParams(dimension_semantics=sem, vmem_limit_bytes=VMEM_LIMIT_BYTES)


def _sigmoid(x):
    return 1.0 / (1.0 + jnp.exp(-x))


def _silu(x):
    return x * _sigmoid(x)


def _gelu(x):
    return 0.5 * x * (1.0 + lax.erf(x * np.float32(1.0 / np.sqrt(2.0))))


def _rms_norm_rows(x, w):
    return x * lax.rsqrt(jnp.mean(x * x, axis=-1, keepdims=True) + EPS) * w


def _cast_kernel(x_ref, o_ref):
    o_ref[...] = x_ref[...].astype(o_ref.dtype)


def _cast_bf16(w):
    l, r, c = w.shape
    w2 = w.reshape(l * r, c)
    return pl.pallas_call(
        _cast_kernel,
        grid=(l * r // CAST_BM,),
        in_specs=[pl.BlockSpec((CAST_BM, c), lambda i: (i, 0))],
        out_specs=pl.BlockSpec((CAST_BM, c), lambda i: (i, 0)),
        out_shape=jax.ShapeDtypeStruct((l * r, c), BF16),
        compiler_params=_params("parallel"),
        name="cast_bf16",
    )(w2)


def _norm_kernel(x_ref, w_ref, o_ref):
    o_ref[...] = _rms_norm_rows(x_ref[...], w_ref[...]).astype(o_ref.dtype)


def _rms_norm(x, w, out_dtype):
    n, d = x.shape
    return pl.pallas_call(
        _norm_kernel,
        grid=(n // ROW_BM,),
        in_specs=[pl.BlockSpec((ROW_BM, d), lambda i: (i, 0)),
                  pl.BlockSpec((1, d), lambda i: (0, 0))],
        out_specs=pl.BlockSpec((ROW_BM, d), lambda i: (i, 0)),
        out_shape=jax.ShapeDtypeStruct((n, d), out_dtype),
        compiler_params=_params("parallel"),
        name="rms_norm",
    )(x, w.reshape(1, d))


def _in_proj_kernel(h_ref, w_ref, cos_ref, sin_ref, z_ref, wbf_ref):
    j = pl.program_id(0)

    @pl.when(pl.program_id(1) == 0)
    def _():
        wbf_ref[...] = w_ref[...].astype(BF16)

    def emit(fn):
        for r in range(IN_BM // MXU_TILE):
            rows = slice(r * MXU_TILE, (r + 1) * MXU_TILE)
            for c in range(IN_BN // MXU_TILE):
                cols = slice(c * MXU_TILE, (c + 1) * MXU_TILE)
                a = jnp.dot(h_ref[rows, :], wbf_ref[:, cols], preferred_element_type=F32)
                z_ref[rows, cols] = fn(a, rows).astype(z_ref.dtype)

    def rotary(scale):
        def fn(a, rows):
            cos, sin = cos_ref[rows, :], sin_ref[rows, :]
            heads = []
            for hh in range(MXU_TILE // RET_DK):
                t = a[:, hh * RET_DK:(hh + 1) * RET_DK]
                r = t * cos + pltpu.roll(t, RET_DK // 2, axis=1) * sin
                heads.append(r * scale if scale != 1.0 else r)
            return jnp.concatenate(heads, axis=1)
        return fn

    @pl.when(j == _SEG_Q)
    def _():
        emit(rotary(np.float32(RET_DK ** -0.5)))

    @pl.when(j == _SEG_K)
    def _():
        emit(rotary(1.0))

    @pl.when((j >= _SEG_V) & (j < _SEG_G))
    def _():
        emit(lambda a, rows: a)

    @pl.when((j >= _SEG_G) & (j < _SEG_SU))
    def _():
        emit(lambda a, rows: _silu(a))

    @pl.when((j >= _SEG_SU) & (j < _SEG_GATE))
    def _():
        emit(lambda a, rows: _gelu(a))

    @pl.when(j >= _SEG_GATE)
    def _():
        emit(lambda a, rows: _sigmoid(a))


def _in_proj(h, w_in, layer, cos2, sin2, seq):
    n = h.shape[0]
    pos_blocks = seq // IN_BM
    return pl.pallas_call(
        _in_proj_kernel,
        grid=(IN_COLS // IN_BN, n // IN_BM),
        in_specs=[
            pl.BlockSpec((IN_BM, D_MODEL), lambda j, m: (m, 0)),
            pl.BlockSpec((None, D_MODEL, IN_BN), lambda j, m: (layer, 0, j)),
            pl.BlockSpec((IN_BM, RET_DK), lambda j, m: (m % pos_blocks, 0)),
            pl.BlockSpec((IN_BM, RET_DK), lambda j, m: (m % pos_blocks, 0)),
        ],
        out_specs=pl.BlockSpec((IN_BM, IN_BN), lambda j, m: (m, j)),
        out_shape=jax.ShapeDtypeStruct((n, IN_COLS), BF16),
        scratch_shapes=[pltpu.VMEM((D_MODEL, IN_BN), BF16)],
        compiler_params=_params("arbitrary", "arbitrary"),
        name="in_proj",
    )(h, w_in, cos2, sin2)


def _retention_kernel(q_ref, k_ref, v_ref, g_ref, dmask_ref, qdec_ref, kdec_ref, cdec_ref,
                      gnw_ref, o_ref, state_ref):
    @pl.when(pl.program_id(2) == 0)
    def _():
        state_ref[...] = jnp.zeros_like(state_ref)

    def step(t, carry):
        rows = pl.ds(pl.multiple_of(t * RET_T, RET_T), RET_T)
        for hh in range(RET_HB):
            qk_cols = slice(hh * RET_DK, (hh + 1) * RET_DK)
            v_cols = slice(hh * RET_DV, (hh + 1) * RET_DV)
            q = q_ref[rows, qk_cols]
            k = k_ref[rows, qk_cols]
            v = v_ref[rows, v_cols]
            s = lax.dot_general(q, k, (((1,), (1,)), ((), ())), preferred_element_type=F32)
            s = s * dmask_ref[hh]
            o = jnp.dot(s.astype(BF16), v, preferred_element_type=F32)
            st = state_ref[hh]
            o = o + qdec_ref[hh] * jnp.dot(q, st.astype(BF16), preferred_element_type=F32)
            kd = (k.astype(F32) * kdec_ref[hh]).astype(BF16)
            upd = lax.dot_general(kd, v, (((0,), (0,)), ((), ())), preferred_element_type=F32)
            state_ref[hh] = st * cdec_ref[hh] + upd
            mu = jnp.mean(o, axis=-1, keepdims=True)
            d = o - mu
            var = jnp.mean(d * d, axis=-1, keepdims=True)
            y = d * lax.rsqrt(var + EPS) * gnw_ref[:, v_cols] * g_ref[rows, v_cols].astype(F32)
            o_ref[rows, v_cols] = y.astype(o_ref.dtype)
        return carry

    lax.fori_loop(0, RET_TS // RET_T, step, 0)


def _retention(z, gn_w, consts, batch, seq):
    dmask, qdec, kdec, cdec = consts
    n = z.shape[0]
    sb = seq // RET_TS
    hg = RET_HEADS // RET_HB
    qk_w, v_w = RET_HB * RET_DK, RET_HB * RET_DV
    k_off = RET_QK // qk_w
    v_off = 2 * RET_QK // v_w
    g_off = v_off + hg
    head_const = lambda r, c: pl.BlockSpec((RET_HB, r, c), lambda b, h, s: (h, 0, 0))
    return pl.pallas_call(
        _retention_kernel,
        grid=(batch, hg, sb),
        in_specs=[
            pl.BlockSpec((RET_TS, qk_w), lambda b, h, s: (b * sb + s, h)),
            pl.BlockSpec((RET_TS, qk_w), lambda b, h, s: (b * sb + s, k_off + h)),
            pl.BlockSpec((RET_TS, v_w), lambda b, h, s: (b * sb + s, v_off + h)),
            pl.BlockSpec((RET_TS, v_w), lambda b, h, s: (b * sb + s, g_off + h)),
            head_const(RET_T, RET_T),
            head_const(RET_T, RET_DV),
            head_const(RET_T, RET_DK),
            head_const(RET_DK, RET_DV),
            pl.BlockSpec((1, v_w), lambda b, h, s: (0, h)),
        ],
        out_specs=pl.BlockSpec((RET_TS, v_w), lambda b, h, s: (b * sb + s, h)),
        out_shape=jax.ShapeDtypeStruct((n, RET_V), BF16),
        scratch_shapes=[pltpu.VMEM((RET_HB, RET_DK, RET_DV), F32)],
        compiler_params=_params("parallel", "parallel", "arbitrary"),
        name="retention",
    )(z, z, z, z, dmask, qdec, kdec, cdec, gn_w.reshape(1, RET_V))


def _retention_consts():
    log_g = jnp.log1p(-(2.0 ** (-5.0 - jnp.arange(RET_HEADS, dtype=F32))))
    idx = jnp.arange(RET_T, dtype=F32)
    dist = jnp.abs(idx[:, None] - idx[None, :])
    chunk = jnp.arange(RET_T) // CHUNK
    visible = chunk[None, :] <= chunk[:, None]
    dmask = jnp.where(visible[None], jnp.exp(log_g[:, None, None] * dist), 0.0)
    qdec = jnp.exp(log_g[:, None] * (idx[None, :] + 1.0))
    kdec = jnp.exp(log_g[:, None] * (RET_T - 1.0 - idx[None, :]))
    cdec = jnp.exp(log_g * RET_T)
    qdec = jnp.broadcast_to(qdec[:, :, None], (RET_HEADS, RET_T, RET_DV))
    kdec = jnp.broadcast_to(kdec[:, :, None], (RET_HEADS, RET_T, RET_DK))
    cdec = jnp.broadcast_to(cdec[:, None, None], (RET_HEADS, RET_DK, RET_DV))
    return dmask.astype(F32), qdec.astype(F32), kdec.astype(F32), cdec.astype(F32)


def _sgu_kernel(zu_ref, zv_ref, lnw_ref, lnb_ref, ws_ref, bs_ref, o_ref):
    zv = zv_ref[...].astype(F32)
    mu = jnp.mean(zv, axis=-1, keepdims=True)
    d = zv - mu
    var = jnp.mean(d * d, axis=-1, keepdims=True)
    vn = (d * lax.rsqrt(var + EPS) * lnw_ref[...] + lnb_ref[...]).astype(BF16)
    row_chunk = lax.broadcasted_iota(jnp.int32, (SGU_LEN, SGU_LEN), 0) // CHUNK
    col_chunk = lax.broadcasted_iota(jnp.int32, (SGU_LEN, SGU_LEN), 1) // CHUNK
    visible = col_chunk <= row_chunk
    for g in range(SGU_GROUPS):
        wm = jnp.where(visible, ws_ref[g], 0.0).astype(BF16)
        cols = slice(g * SGU_CH, (g + 1) * SGU_CH)
        for r in range(SGU_BM // SGU_LEN):
            rows = slice(r * SGU_LEN, (r + 1) * SGU_LEN)
            mixed = jnp.dot(wm, vn[rows, cols], preferred_element_type=F32) + bs_ref[g]
            o_ref[rows, cols] = (zu_ref[rows, cols].astype(F32) * mixed).astype(o_ref.dtype)


def _sgu(z, ln_w, ln_b, w_s, b_s):
    n = z.shape[0]
    su_off = (2 * RET_QK + 2 * RET_V) // D_MODEL
    bs_b = jnp.broadcast_to(b_s[:, :, None], (SGU_GROUPS, SGU_LEN, SGU_CH))
    return pl.pallas_call(
        _sgu_kernel,
        grid=(n // SGU_BM,),
        in_specs=[
            pl.BlockSpec((SGU_BM, D_MODEL), lambda i: (i, su_off)),
            pl.BlockSpec((SGU_BM, D_MODEL), lambda i: (i, su_off + 1)),
            pl.BlockSpec((1, D_MODEL), lambda i: (0, 0)),
            pl.BlockSpec((1, D_MODEL), lambda i: (0, 0)),
            pl.BlockSpec((SGU_GROUPS, SGU_LEN, SGU_LEN), lambda i: (0, 0, 0)),
            pl.BlockSpec((SGU_GROUPS, SGU_LEN, SGU_CH), lambda i: (0, 0, 0)),
        ],
        out_specs=pl.BlockSpec((SGU_BM, D_MODEL), lambda i: (i, 0)),
        out_shape=jax.ShapeDtypeStruct((n, D_MODEL), BF16),
        compiler_params=_params("parallel"),
        name="spatial_gating",
    )(z, z, ln_w.reshape(1, D_MODEL), ln_b.reshape(1, D_MODEL), w_s, bs_b)


def _merge_kernel(a_ref, b_ref, ga_ref, gb_ref, x_ref, wa_ref, wb_ref, wo_ref, nw_ref,
                  xo_ref, ho_ref):
    pa = jnp.dot(a_ref[...], wa_ref[...], preferred_element_type=F32)
    pb = jnp.dot(b_ref[...], wb_ref[...], preferred_element_type=F32)
    merged = ga_ref[...].astype(F32) * pa + gb_ref[...].astype(F32) * pb
    x = x_ref[...] + jnp.dot(merged.astype(BF16), wo_ref[...], preferred_element_type=F32)
    xo_ref[...] = x
    ho_ref[...] = _rms_norm_rows(x, nw_ref[...]).astype(ho_ref.dtype)


def _resident(shape, index_map):
    return pl.BlockSpec(shape, index_map, pipeline_mode=pl.Buffered(1))


def _merge(a, b, z, x, wa, wb, wo, layer, norm_w):
    n = x.shape[0]
    gate_off = (2 * RET_QK + 2 * RET_V + 2 * D_MODEL) // D_MODEL
    row = lambda c: pl.BlockSpec((ROW_BM, D_MODEL), lambda i: (i, c))
    weight = lambda: _resident((D_MODEL, D_MODEL), lambda i: (layer, 0))
    return pl.pallas_call(
        _merge_kernel,
        grid=(n // ROW_BM,),
        in_specs=[row(0), row(0), row(gate_off), row(gate_off + 1), row(0),
                  weight(), weight(), weight(),
                  pl.BlockSpec((1, D_MODEL), lambda i: (0, 0))],
        out_specs=[row(0), row(0)],
        out_shape=[jax.ShapeDtypeStruct((n, D_MODEL), F32),
                   jax.ShapeDtypeStruct((n, D_MODEL), BF16)],
        compiler_params=_params("parallel"),
        name="merge_out_proj",
    )(a, b, z, z, x, wa, wb, wo, norm_w.reshape(1, D_MODEL))


def _ffn_in_kernel(h_ref, wa_ref, wc_ref, u_ref, wabf_ref, wcbf_ref):
    @pl.when(pl.program_id(1) == 0)
    def _():
        wabf_ref[...] = wa_ref[...].astype(BF16)
        wcbf_ref[...] = wc_ref[...].astype(BF16)

    for r in range(FFN_BM // MXU_TILE):
        rows = slice(r * MXU_TILE, (r + 1) * MXU_TILE)
        for c in range(FFN_BN // MXU_TILE):
            cols = slice(c * MXU_TILE, (c + 1) * MXU_TILE)
            a = jnp.dot(h_ref[rows, :], wabf_ref[:, cols], preferred_element_type=F32)
            g = jnp.dot(h_ref[rows, :], wcbf_ref[:, cols], preferred_element_type=F32)
            u_ref[rows, cols] = (_silu(a) * g).astype(u_ref.dtype)


def _ffn_in(h, w_ffn_in, layer):
    n = h.shape[0]
    nb = D_FF // FFN_BN
    return pl.pallas_call(
        _ffn_in_kernel,
        grid=(nb, n // FFN_BM),
        in_specs=[
            pl.BlockSpec((FFN_BM, D_MODEL), lambda j, m: (m, 0)),
            pl.BlockSpec((None, D_MODEL, FFN_BN), lambda j, m: (layer, 0, j)),
            pl.BlockSpec((None, D_MODEL, FFN_BN), lambda j, m: (layer, 0, nb + j)),
        ],
        out_specs=pl.BlockSpec((FFN_BM, FFN_BN), lambda j, m: (m, j)),
        out_shape=jax.ShapeDtypeStruct((n, D_FF), BF16),
        scratch_shapes=[pltpu.VMEM((D_MODEL, FFN_BN), BF16),
                        pltpu.VMEM((D_MODEL, FFN_BN), BF16)],
        compiler_params=_params("arbitrary", "arbitrary"),
        name="ffn_in",
    )(h, w_ffn_in, w_ffn_in)


def _ffn_out_kernel(u_ref, x_ref, w_ref, nw_ref, xo_ref, ho_ref):
    x = x_ref[...] + jnp.dot(u_ref[...], w_ref[...], preferred_element_type=F32)
    xo_ref[...] = x
    ho_ref[...] = _rms_norm_rows(x, nw_ref[...]).astype(ho_ref.dtype)


def _ffn_out(u, x, w, layer, norm_w, h_dtype):
    n = x.shape[0]
    row = lambda width: pl.BlockSpec((ROW_BM, width), lambda i: (i, 0))
    return pl.pallas_call(
        _ffn_out_kernel,
        grid=(n // ROW_BM,),
        in_specs=[row(D_FF), row(D_MODEL),
                  _resident((D_FF, D_MODEL), lambda i: (layer, 0)),
                  pl.BlockSpec((1, D_MODEL), lambda i: (0, 0))],
        out_specs=[row(D_MODEL), row(D_MODEL)],
        out_shape=[jax.ShapeDtypeStruct((n, D_MODEL), F32),
                   jax.ShapeDtypeStruct((n, D_MODEL), h_dtype)],
        compiler_params=_params("parallel"),
        name="ffn_out",
    )(u, x, w, norm_w.reshape(1, D_MODEL))


def _rotary_tables(seq):
    half = RET_DK // 2
    inv = ROPE_BASE ** (-jnp.arange(half, dtype=F32) / half)
    ang = jnp.arange(seq, dtype=jnp.int32).astype(F32)[:, None] * inv[None, :]
    cos, sin = jnp.cos(ang), jnp.sin(ang)
    return jnp.concatenate([cos, cos], axis=1), jnp.concatenate([-sin, sin], axis=1)


def kernel(x, norm_mix_w, w_in, ret_gn_w, ret_proj, sgu_ln_w, sgu_ln_b, sgu_w_s, sgu_b_s,
           sgu_proj, w_out, norm_ffn_w, w_ffn_in, w_ffn_out, final_norm_w):
    batch, seq, d = x.shape
    assert d == D_MODEL and seq % IN_BM == 0 and seq % RET_TS == 0 and RET_TS % RET_T == 0
    n = batch * seq
    cos2, sin2 = _rotary_tables(seq)
    ret_consts = _retention_consts()
    ret_proj_bf = _cast_bf16(ret_proj)
    sgu_proj_bf = _cast_bf16(sgu_proj)
    w_out_bf = _cast_bf16(w_out)
    w_ffn_out_bf = _cast_bf16(w_ffn_out)

    xf = x.reshape(n, d)
    h = _rms_norm(xf, norm_mix_w[0], BF16)
    for l in range(DEPTH):
        z = _in_proj(h, w_in, l, cos2, sin2, seq)
        ret = _retention(z, ret_gn_w[l], ret_consts, batch, seq)
        sgu = _sgu(z, sgu_ln_w[l], sgu_ln_b[l], sgu_w_s[l], sgu_b_s[l])
        xf, h = _merge(ret, sgu, z, xf, ret_proj_bf, sgu_proj_bf, w_out_bf, l, norm_ffn_w[l])
        u = _ffn_in(h, w_ffn_in, l)
        last = l == DEPTH - 1
        next_w = final_norm_w if last else norm_mix_w[l + 1]
        xf, h = _ffn_out(u, xf, w_ffn_out_bf, l, next_w, F32 if last else BF16)
    return h.reshape(batch, seq, d)
```

```python
import functools

import jax
import jax.numpy as jnp
import numpy as np
from jax import lax
from jax.experimental import pallas as pl
from jax.experimental.pallas import tpu as pltpu

D_MODEL = 2048
DEPTH = 4
CHUNK = 64
RET_HEADS = 8
RET_DK = 128
RET_DV = 256
RET_QK = RET_HEADS * RET_DK
RET_V = RET_HEADS * RET_DV
SGU_GROUPS = 8
SGU_LEN = 128
SGU_CH = D_MODEL // SGU_GROUPS
D_FF = 5632
ROPE_BASE = 10000.0
EPS = 1e-6

COL_QK = 0
COL_V = 2 * RET_QK
COL_G = COL_V + RET_V
COL_S = COL_G + RET_V
COL_GATE = COL_S + 2 * D_MODEL

F32 = jnp.float32
BF16 = jnp.bfloat16

V7X_VMEM_BYTES = 64 * 2**20
VMEM_LIMIT_BYTES = V7X_VMEM_BYTES - 8 * 2**20
MXU_TILE = 256

IN_BM, IN_BN = 2048, 1024
FFN_BM, FFN_BN = 2048, 512
MG_BM, MG_BN = 1024, 512
OUT_BM = 512
ROW_BM = 256
RET_T = 256
RET_TS = 2048
RET_HB = 4
SGU_BM = 512
CAST_BM = 512


def _params(*sem):
    return pltpu.CompilerParams(dimension_semantics=sem, vmem_limit_bytes=VMEM_LIMIT_BYTES)


def _sigmoid(x):
    return 1.0 / (1.0 + jnp.exp(-x))


def _silu(x):
    return x * _sigmoid(x)


def _gelu(x):
    return 0.5 * x * (1.0 + lax.erf(x * np.float32(1.0 / np.sqrt(2.0))))


def _rms_norm_rows(x, w):
    return x * lax.rsqrt(jnp.mean(x * x, axis=-1, keepdims=True) + EPS) * w


def _resident(shape, index_map):
    return pl.BlockSpec(shape, index_map, pipeline_mode=pl.Buffered(1))


def _tile_slices(n, size=MXU_TILE):
    return [slice(i * size, (i + 1) * size) for i in range(n // size)]


def _cast_kernel(x_ref, o_ref):
    o_ref[...] = x_ref[...].astype(o_ref.dtype)


def _cast_bf16(w):
    l, r, c = w.shape
    w2 = w.reshape(l * r, c)
    return pl.pallas_call(
        _cast_kernel,
        grid=(l * r // CAST_BM,),
        in_specs=[pl.BlockSpec((CAST_BM, c), lambda i: (i, 0))],
        out_specs=pl.BlockSpec((CAST_BM, c), lambda i: (i, 0)),
        out_shape=jax.ShapeDtypeStruct((l * r, c), BF16),
        compiler_params=_params("parallel"),
        name="cast_bf16",
    )(w2)


def _norm_kernel(x_ref, w_ref, o_ref):
    o_ref[...] = _rms_norm_rows(x_ref[...], w_ref[...]).astype(o_ref.dtype)


def _rms_norm(x, w, out_dtype):
    n, d = x.shape
    return pl.pallas_call(
        _norm_kernel,
        grid=(n // ROW_BM,),
        in_specs=[pl.BlockSpec((ROW_BM, d), lambda i: (i, 0)),
                  pl.BlockSpec((1, d), lambda i: (0, 0))],
        out_specs=pl.BlockSpec((ROW_BM, d), lambda i: (i, 0)),
        out_shape=jax.ShapeDtypeStruct((n, d), out_dtype),
        compiler_params=_params("parallel"),
        name="rms_norm",
    )(x, w.reshape(1, d))


def _cast_weights_once(pairs):
    @pl.when(pl.program_id(1) == 0)
    def _():
        for w_ref, wbf_ref in pairs:
            wbf_ref[...] = w_ref[...].astype(BF16)


def _proj_units(h_ref, wbf_ref, z_ref, epilogue):
    for rows in _tile_slices(z_ref.shape[0]):
        for cols in _tile_slices(z_ref.shape[1]):
            a = jnp.dot(h_ref[rows, :], wbf_ref[:, cols], preferred_element_type=F32)
            z_ref[rows, cols] = epilogue(a, rows, cols).astype(z_ref.dtype)


def _proj_act_kernel(h_ref, w_ref, z_ref, wbf_ref, *, act):
    _cast_weights_once([(w_ref, wbf_ref)])
    _proj_units(h_ref, wbf_ref, z_ref, lambda a, rows, cols: act(a))


def _proj_rotary_kernel(h_ref, w_ref, cos_ref, sin_ref, z_ref, wbf_ref):
    _cast_weights_once([(w_ref, wbf_ref)])
    scale = jnp.where(pl.program_id(0) == 0, np.float32(RET_DK ** -0.5), np.float32(1.0))

    def rotary(a, rows, cols):
        cos, sin = cos_ref[rows, :] * scale, sin_ref[rows, :] * scale
        heads = []
        for hh in range(MXU_TILE // RET_DK):
            t = a[:, hh * RET_DK:(hh + 1) * RET_DK]
            heads.append(t * cos + pltpu.roll(t, RET_DK // 2, axis=1) * sin)
        return jnp.concatenate(heads, axis=1)

    _proj_units(h_ref, wbf_ref, z_ref, rotary)


def _in_proj(kernel_fn, name, h, w_in, layer, col0, width, extra=(), extra_specs=()):
    n = h.shape[0]
    assert col0 % IN_BN == 0 and width % IN_BN == 0
    cb0 = col0 // IN_BN
    return pl.pallas_call(
        kernel_fn,
        grid=(width // IN_BN, n // IN_BM),
        in_specs=[
            pl.BlockSpec((IN_BM, D_MODEL), lambda j, m: (m, 0)),
            pl.BlockSpec((None, D_MODEL, IN_BN), lambda j, m: (layer, 0, cb0 + j)),
            *extra_specs,
        ],
        out_specs=pl.BlockSpec((IN_BM, IN_BN), lambda j, m: (m, j)),
        out_shape=jax.ShapeDtypeStruct((n, width), BF16),
        scratch_shapes=[pltpu.VMEM((D_MODEL, IN_BN), BF16)],
        compiler_params=_params("arbitrary", "arbitrary"),
        name=name,
    )(h, w_in, *extra)


def _in_proj_all(h, w_in, layer, cos2, sin2, seq):
    assert IN_BN == RET_QK
    pos_blocks = seq // IN_BM
    table = pl.BlockSpec((IN_BM, RET_DK), lambda j, m: (m % pos_blocks, 0))
    zqk = _in_proj(_proj_rotary_kernel, "in_proj_qk", h, w_in, layer, COL_QK, 2 * RET_QK,
                   (cos2, sin2), (table, table))
    zv = _in_proj(functools.partial(_proj_act_kernel, act=lambda a: a), "in_proj_v",
                  h, w_in, layer, COL_V, RET_V)
    zg = _in_proj(functools.partial(_proj_act_kernel, act=_silu), "in_proj_g",
                  h, w_in, layer, COL_G, RET_V)
    zs = _in_proj(functools.partial(_proj_act_kernel, act=_gelu), "in_proj_s",
                  h, w_in, layer, COL_S, 2 * D_MODEL)
    zgate = _in_proj(functools.partial(_proj_act_kernel, act=_sigmoid), "in_proj_gate",
                     h, w_in, layer, COL_GATE, 2 * D_MODEL)
    return zqk, zv, zg, zs, zgate


def _retention_kernel(q_ref, k_ref, v_ref, g_ref, dmask_ref, qdec_ref, kdec_ref, cdec_ref,
                      o_ref, state_ref, acc_ref):
    @pl.when(pl.program_id(2) == 0)
    def _():
        state_ref[...] = jnp.zeros_like(state_ref)

    heads = range(RET_HB)
    qk_cols = [slice(hh * RET_DK, (hh + 1) * RET_DK) for hh in heads]
    v_cols = [slice(hh * RET_DV, (hh + 1) * RET_DV) for hh in heads]
    contract_last = (((1,), (1,)), ((), ()))
    contract_first = (((0,), (0,)), ((), ()))

    def chunk_rows(t):
        return pl.ds(pl.multiple_of(t * RET_T, RET_T), RET_T)

    def matmul_stage(t):
        rows = chunk_rows(t)
        q = [q_ref[rows, c] for c in qk_cols]
        k = [k_ref[rows, c] for c in qk_cols]
        v = [v_ref[rows, c] for c in v_cols]
        s = [lax.dot_general(q[hh], k[hh], contract_last, preferred_element_type=F32)
             for hh in heads]
        st = [state_ref[hh] for hh in heads]
        kd = [(k[hh].astype(F32) * kdec_ref[hh]).astype(BF16) for hh in heads]
        upd = [lax.dot_general(kd[hh], v[hh], contract_first, preferred_element_type=F32)
               for hh in heads]
        for hh in heads:
            state_ref[hh] = st[hh] * cdec_ref[hh] + upd[hh]
        p = [(s[hh] * dmask_ref[hh]).astype(BF16) for hh in heads]
        qs = [(q[hh].astype(F32) * qdec_ref[hh]).astype(BF16) for hh in heads]
        for hh in heads:
            acc_ref[hh] = (jnp.dot(p[hh], v[hh], preferred_element_type=F32)
                           + jnp.dot(qs[hh], st[hh].astype(BF16), preferred_element_type=F32))

    def norm_stage(t):
        rows = chunk_rows(t)
        for hh in heads:
            o = acc_ref[hh]
            mu = jnp.mean(o, axis=-1, keepdims=True)
            d = o - mu
            var = jnp.mean(d * d, axis=-1, keepdims=True)
            y = d * lax.rsqrt(var + EPS) * g_ref[rows, v_cols[hh]].astype(F32)
            o_ref[rows, v_cols[hh]] = y.astype(o_ref.dtype)

    n_chunks = RET_TS // RET_T
    matmul_stage(0)

    def body(t, carry):
        norm_stage(t - 1)
        matmul_stage(t)
        return carry

    lax.fori_loop(1, n_chunks, body, 0)
    norm_stage(n_chunks - 1)


def _retention(zqk, zv, zg, consts, batch, seq):
    dmask, qdec, kdec, cdec = consts
    n = zqk.shape[0]
    sb = seq // RET_TS
    hg = RET_HEADS // RET_HB
    qk_w, v_w = RET_HB * RET_DK, RET_HB * RET_DV
    head_const = lambda r, c: pl.BlockSpec((RET_HB, r, c), lambda b, h, s: (h, 0, 0))
    v_spec = pl.BlockSpec((RET_TS, v_w), lambda b, h, s: (b * sb + s, h))
    return pl.pallas_call(
        _retention_kernel,
        grid=(batch, hg, sb),
        in_specs=[
            pl.BlockSpec((RET_TS, qk_w), lambda b, h, s: (b * sb + s, h)),
            pl.BlockSpec((RET_TS, qk_w), lambda b, h, s: (b * sb + s, hg + h)),
            v_spec,
            v_spec,
            head_const(RET_T, RET_T),
            head_const(RET_T, RET_DK),
            head_const(RET_T, RET_DK),
            head_const(RET_DK, RET_DV),
        ],
        out_specs=v_spec,
        out_shape=jax.ShapeDtypeStruct((n, RET_V), BF16),
        scratch_shapes=[pltpu.VMEM((RET_HB, RET_DK, RET_DV), F32),
                        pltpu.VMEM((RET_HB, RET_T, RET_DV), F32)],
        compiler_params=_params("parallel", "parallel", "arbitrary"),
        name="retention",
    )(zqk, zqk, zv, zg, dmask, qdec, kdec, cdec)


def _retention_consts():
    log_g = jnp.log1p(-(2.0 ** (-5.0 - jnp.arange(RET_HEADS, dtype=F32))))
    idx = jnp.arange(RET_T, dtype=F32)
    dist = jnp.abs(idx[:, None] - idx[None, :])
    chunk = jnp.arange(RET_T) // CHUNK
    visible = chunk[None, :] <= chunk[:, None]
    dmask = jnp.where(visible[None], jnp.exp(log_g[:, None, None] * dist), 0.0)
    qdec = jnp.exp(log_g[:, None] * (idx[None, :] + 1.0))
    kdec = jnp.exp(log_g[:, None] * (RET_T - 1.0 - idx[None, :]))
    cdec = jnp.exp(log_g * RET_T)
    qdec = jnp.broadcast_to(qdec[:, :, None], (RET_HEADS, RET_T, RET_DK))
    kdec = jnp.broadcast_to(kdec[:, :, None], (RET_HEADS, RET_T, RET_DK))
    cdec = jnp.broadcast_to(cdec[:, None, None], (RET_HEADS, RET_DK, RET_DV))
    return dmask.astype(F32), qdec.astype(F32), kdec.astype(F32), cdec.astype(F32)


def _sgu_kernel(zu_ref, zv_ref, lnw_ref, lnb_ref, ws_ref, bs_ref, o_ref):
    zv = zv_ref[...].astype(F32)
    mu = jnp.mean(zv, axis=-1, keepdims=True)
    d = zv - mu
    var = jnp.mean(d * d, axis=-1, keepdims=True)
    vn = (d * lax.rsqrt(var + EPS) * lnw_ref[...] + lnb_ref[...]).astype(BF16)
    row_chunk = lax.broadcasted_iota(jnp.int32, (SGU_LEN, SGU_LEN), 0) // CHUNK
    col_chunk = lax.broadcasted_iota(jnp.int32, (SGU_LEN, SGU_LEN), 1) // CHUNK
    visible = col_chunk <= row_chunk
    for g in range(SGU_GROUPS):
        wm = jnp.where(visible, ws_ref[g], 0.0).astype(BF16)
        cols = slice(g * SGU_CH, (g + 1) * SGU_CH)
        for rows in _tile_slices(SGU_BM, SGU_LEN):
            mixed = jnp.dot(wm, vn[rows, cols], preferred_element_type=F32) + bs_ref[g]
            o_ref[rows, cols] = (zu_ref[rows, cols].astype(F32) * mixed).astype(o_ref.dtype)


def _sgu(zs, ln_w, ln_b, w_s, b_s):
    n = zs.shape[0]
    bs_b = jnp.broadcast_to(b_s[:, :, None], (SGU_GROUPS, SGU_LEN, SGU_CH))
    return pl.pallas_call(
        _sgu_kernel,
        grid=(n // SGU_BM,),
        in_specs=[
            pl.BlockSpec((SGU_BM, D_MODEL), lambda i: (i, 0)),
            pl.BlockSpec((SGU_BM, D_MODEL), lambda i: (i, 1)),
            pl.BlockSpec((1, D_MODEL), lambda i: (0, 0)),
            pl.BlockSpec((1, D_MODEL), lambda i: (0, 0)),
            pl.BlockSpec((SGU_GROUPS, SGU_LEN, SGU_LEN), lambda i: (0, 0, 0)),
            pl.BlockSpec((SGU_GROUPS, SGU_LEN, SGU_CH), lambda i: (0, 0, 0)),
        ],
        out_specs=pl.BlockSpec((SGU_BM, D_MODEL), lambda i: (i, 0)),
        out_shape=jax.ShapeDtypeStruct((n, D_MODEL), BF16),
        compiler_params=_params("parallel"),
        name="spatial_gating",
    )(zs, zs, ln_w.reshape(1, D_MODEL), ln_b.reshape(1, D_MODEL), w_s, bs_b)


def _branch_merge_kernel(a_ref, b_ref, ga_ref, gb_ref, wa_ref, wb_ref, gain_ref, o_ref,
                         wabf_ref, wbbf_ref):
    @pl.when(pl.program_id(1) == 0)
    def _():
        wabf_ref[...] = (wa_ref[...] * gain_ref[...]).astype(BF16)
        wbbf_ref[...] = wb_ref[...].astype(BF16)

    for rows in _tile_slices(MG_BM):
        for cols in _tile_slices(MG_BN):
            pa = jnp.dot(a_ref[rows, :], wabf_ref[:, cols], preferred_element_type=F32)
            pb = jnp.dot(b_ref[rows, :], wbbf_ref[:, cols], preferred_element_type=F32)
            merged = ga_ref[rows, cols].astype(F32) * pa + gb_ref[rows, cols].astype(F32) * pb
            o_ref[rows, cols] = merged.astype(o_ref.dtype)


def _branch_merge(a, b, zgate, wa, wb, layer, a_gain):
    n = a.shape[0]
    nb = D_MODEL // MG_BN
    act = pl.BlockSpec((MG_BM, D_MODEL), lambda j, m: (m, 0))
    weight = pl.BlockSpec((None, D_MODEL, MG_BN), lambda j, m: (layer, 0, j))
    return pl.pallas_call(
        _branch_merge_kernel,
        grid=(nb, n // MG_BM),
        in_specs=[act, act,
                  pl.BlockSpec((MG_BM, MG_BN), lambda j, m: (m, j)),
                  pl.BlockSpec((MG_BM, MG_BN), lambda j, m: (m, nb + j)),
                  weight, weight,
                  pl.BlockSpec((D_MODEL, 1), lambda j, m: (0, 0))],
        out_specs=pl.BlockSpec((MG_BM, MG_BN), lambda j, m: (m, j)),
        out_shape=jax.ShapeDtypeStruct((n, D_MODEL), BF16),
        scratch_shapes=[pltpu.VMEM((D_MODEL, MG_BN), BF16), pltpu.VMEM((D_MODEL, MG_BN), BF16)],
        compiler_params=_params("arbitrary", "arbitrary"),
        name="branch_merge",
    )(a, b, zgate, zgate, wa, wb, a_gain.reshape(RET_V, 1))


def _out_proj_kernel(m_ref, x_ref, w_ref, nw_ref, xo_ref, ho_ref, wbf_ref):
    @pl.when(pl.program_id(0) == 0)
    def _():
        wbf_ref[...] = w_ref[...].astype(BF16)

    for rows in _tile_slices(OUT_BM):
        x = x_ref[rows, :] + jnp.dot(m_ref[rows, :], wbf_ref[...], preferred_element_type=F32)
        xo_ref[rows, :] = x
        ho_ref[rows, :] = _rms_norm_rows(x, nw_ref[...]).astype(ho_ref.dtype)


def _out_proj(merged, x, w_out, layer, norm_w):
    n = x.shape[0]
    row = pl.BlockSpec((OUT_BM, D_MODEL), lambda i: (i, 0))
    return pl.pallas_call(
        _out_proj_kernel,
        grid=(n // OUT_BM,),
        in_specs=[row, row,
                  _resident((None, D_MODEL, D_MODEL), lambda i: (layer, 0, 0)),
                  pl.BlockSpec((1, D_MODEL), lambda i: (0, 0))],
        out_specs=[row, row],
        out_shape=[jax.ShapeDtypeStruct((n, D_MODEL), F32),
                   jax.ShapeDtypeStruct((n, D_MODEL), BF16)],
        scratch_shapes=[pltpu.VMEM((D_MODEL, D_MODEL), BF16)],
        compiler_params=_params("arbitrary"),
        name="out_proj",
    )(merged, x, w_out, norm_w.reshape(1, D_MODEL))


def _ffn_in_kernel(h_ref, wa_ref, wc_ref, u_ref, wabf_ref, wcbf_ref):
    _cast_weights_once([(wa_ref, wabf_ref), (wc_ref, wcbf_ref)])
    for rows in _tile_slices(FFN_BM):
        for cols in _tile_slices(FFN_BN):
            a = jnp.dot(h_ref[rows, :], wabf_ref[:, cols], preferred_element_type=F32)
            g = jnp.dot(h_ref[rows, :], wcbf_ref[:, cols], preferred_element_type=F32)
            u_ref[rows, cols] = (_silu(a) * g).astype(u_ref.dtype)


def _ffn_in(h, w_ffn_in, layer):
    n = h.shape[0]
    nb = D_FF // FFN_BN
    return pl.pallas_call(
        _ffn_in_kernel,
        grid=(nb, n // FFN_BM),
        in_specs=[
            pl.BlockSpec((FFN_BM, D_MODEL), lambda j, m: (m, 0)),
            pl.BlockSpec((None, D_MODEL, FFN_BN), lambda j, m: (layer, 0, j)),
            pl.BlockSpec((None, D_MODEL, FFN_BN), lambda j, m: (layer, 0, nb + j)),
        ],
        out_specs=pl.BlockSpec((FFN_BM, FFN_BN), lambda j, m: (m, j)),
        out_shape=jax.ShapeDtypeStruct((n, D_FF), BF16),
        scratch_shapes=[pltpu.VMEM((D_MODEL, FFN_BN), BF16),
                        pltpu.VMEM((D_MODEL, FFN_BN), BF16)],
        compiler_params=_params("arbitrary", "arbitrary"),
        name="ffn_in",
    )(h, w_ffn_in, w_ffn_in)


def _ffn_out_kernel(u_ref, x_ref, w_ref, nw_ref, xo_ref, ho_ref):
    x = x_ref[...] + jnp.dot(u_ref[...], w_ref[...], preferred_element_type=F32)
    xo_ref[...] = x
    ho_ref[...] = _rms_norm_rows(x, nw_ref[...]).astype(ho_ref.dtype)


def _ffn_out(u, x, w, layer, norm_w, h_dtype):
    n = x.shape[0]
    row = lambda width: pl.BlockSpec((ROW_BM, width), lambda i: (i, 0))
    return pl.pallas_call(
        _ffn_out_kernel,
        grid=(n // ROW_BM,),
        in_specs=[row(D_FF), row(D_MODEL),
                  _resident((D_FF, D_MODEL), lambda i: (layer, 0)),
                  pl.BlockSpec((1, D_MODEL), lambda i: (0, 0))],
        out_specs=[row(D_MODEL), row(D_MODEL)],
        out_shape=[jax.ShapeDtypeStruct((n, D_MODEL), F32),
                   jax.ShapeDtypeStruct((n, D_MODEL), h_dtype)],
        compiler_params=_params("parallel"),
        name="ffn_out",
    )(u, x, w, norm_w.reshape(1, D_MODEL))


def _rotary_tables(seq):
    half = RET_DK // 2
    inv = ROPE_BASE ** (-jnp.arange(half, dtype=F32) / half)
    ang = jnp.arange(seq, dtype=jnp.int32).astype(F32)[:, None] * inv[None, :]
    cos, sin = jnp.cos(ang), jnp.sin(ang)
    return jnp.concatenate([cos, cos], axis=1), jnp.concatenate([-sin, sin], axis=1)


def kernel(x, norm_mix_w, w_in, ret_gn_w, ret_proj, sgu_ln_w, sgu_ln_b, sgu_w_s, sgu_b_s,
           sgu_proj, w_out, norm_ffn_w, w_ffn_in, w_ffn_out, final_norm_w):
    batch, seq, d = x.shape
    assert d == D_MODEL and seq % IN_BM == 0 and seq % RET_TS == 0 and RET_TS % RET_T == 0
    n = batch * seq
    cos2, sin2 = _rotary_tables(seq)
    ret_consts = _retention_consts()
    w_ffn_out_bf = _cast_bf16(w_ffn_out)

    xf = x.reshape(n, d)
    h = _rms_norm(xf, norm_mix_w[0], BF16)
    for l in range(DEPTH):
        zqk, zv, zg, zs, zgate = _in_proj_all(h, w_in, l, cos2, sin2, seq)
        ret = _retention(zqk, zv, zg, ret_consts, batch, seq)
        sgu = _sgu(zs, sgu_ln_w[l], sgu_ln_b[l], sgu_w_s[l], sgu_b_s[l])
        merged = _branch_merge(ret, sgu, zgate, ret_proj, sgu_proj, l, ret_gn_w[l])
        xf, h = _out_proj(merged, xf, w_out, l, norm_ffn_w[l])
        u = _ffn_in(h, w_ffn_in, l)
        last = l == DEPTH - 1
        next_w = final_norm_w if last else norm_mix_w[l + 1]
        xf, h = _ffn_out(u, xf, w_ffn_out_bf, l, next_w, F32 if last else BF16)
    return h.reshape(batch, seq, d)
```

```python
import functools

import jax
import jax.numpy as jnp
import numpy as np
from jax import lax
from jax.experimental import pallas as pl
from jax.experimental.pallas import tpu as pltpu

D_MODEL = 2048
DEPTH = 4
CHUNK = 64
RET_HEADS = 8
RET_DK = 128
RET_DV = 256
RET_QK = RET_HEADS * RET_DK
RET_V = RET_HEADS * RET_DV
SGU_GROUPS = 8
SGU_LEN = 128
SGU_CH = D_MODEL // SGU_GROUPS
D_FF = 5632
ROPE_BASE = 10000.0
EPS = 1e-6

COL_QK = 0
COL_V = 2 * RET_QK
COL_G = COL_V + RET_V
COL_S = COL_G + RET_V
COL_GATE = COL_S + 2 * D_MODEL

F32 = jnp.float32
BF16 = jnp.bfloat16

V7X_VMEM_BYTES = 64 * 2**20
VMEM_LIMIT_BYTES = V7X_VMEM_BYTES - 8 * 2**20
MXU_TILE = 256

IN_BM, IN_BN = 2048, 1024
FFN_BM, FFN_BN = 2048, 512
MG_BM, MG_BN = 1024, 512
OUT_BM = 512
ROW_BM = 256
RET_T = 256
RET_TS = 2048
RET_HB = 4
SGU_BM = 512
CAST_BM = 1024


def _params(*sem):
    return pltpu.CompilerParams(dimension_semantics=sem, vmem_limit_bytes=VMEM_LIMIT_BYTES)


def _sigmoid(x):
    return 1.0 / (1.0 + jnp.exp2(x * np.float32(-np.log2(np.e))))


def _silu(x):
    return x * _sigmoid(x)


def _gelu_x2(x):
    return x * (1.0 + lax.erf(x * np.float32(1.0 / np.sqrt(2.0))))


def _rms_norm_rows(x, w):
    return x * lax.rsqrt(jnp.mean(x * x, axis=-1, keepdims=True) + EPS) * w


def _resident(shape, index_map):
    return pl.BlockSpec(shape, index_map, pipeline_mode=pl.Buffered(1))


def _tile_slices(n, size=MXU_TILE):
    return [slice(i * size, (i + 1) * size) for i in range(n // size)]


def _cast_kernel(x_ref, o_ref):
    o_ref[...] = x_ref[...].astype(o_ref.dtype)


def _cast_bf16(w):
    l, r, c = w.shape
    w2 = w.reshape(l * r, c)
    return pl.pallas_call(
        _cast_kernel,
        grid=(l * r // CAST_BM,),
        in_specs=[pl.BlockSpec((CAST_BM, c), lambda i: (i, 0))],
        out_specs=pl.BlockSpec((CAST_BM, c), lambda i: (i, 0)),
        out_shape=jax.ShapeDtypeStruct((l * r, c), BF16),
        compiler_params=_params("parallel"),
        name="cast_bf16",
    )(w2)


def _norm_kernel(x_ref, w_ref, o_ref):
    o_ref[...] = _rms_norm_rows(x_ref[...], w_ref[...]).astype(o_ref.dtype)


def _rms_norm(x, w, out_dtype):
    n, d = x.shape
    return pl.pallas_call(
        _norm_kernel,
        grid=(n // CAST_BM,),
        in_specs=[pl.BlockSpec((CAST_BM, d), lambda i: (i, 0)),
                  pl.BlockSpec((1, d), lambda i: (0, 0))],
        out_specs=pl.BlockSpec((CAST_BM, d), lambda i: (i, 0)),
        out_shape=jax.ShapeDtypeStruct((n, d), out_dtype),
        compiler_params=_params("parallel"),
        name="rms_norm",
    )(x, w.reshape(1, d))


def _cast_weights_once(pairs):
    @pl.when(pl.program_id(1) == 0)
    def _():
        for w_ref, wbf_ref in pairs:
            wbf_ref[...] = w_ref[...].astype(BF16)


def _proj_units(h_ref, wbf_ref, z_ref, epilogue):
    for rows in _tile_slices(z_ref.shape[0]):
        for cols in _tile_slices(z_ref.shape[1]):
            a = jnp.dot(h_ref[rows, :], wbf_ref[:, cols], preferred_element_type=F32)
            z_ref[rows, cols] = epilogue(a, rows, cols).astype(z_ref.dtype)


def _proj_act_kernel(h_ref, w_ref, z_ref, wbf_ref, *, act):
    _cast_weights_once([(w_ref, wbf_ref)])
    _proj_units(h_ref, wbf_ref, z_ref, lambda a, rows, cols: act(a))


def _proj_act2_kernel(h_ref, w_ref, z_ref, wbf_ref, *, acts, split):
    _cast_weights_once([(w_ref, wbf_ref)])

    @pl.when(pl.program_id(0) < split)
    def _():
        _proj_units(h_ref, wbf_ref, z_ref, lambda a, rows, cols: acts[0](a))

    @pl.when(pl.program_id(0) >= split)
    def _():
        _proj_units(h_ref, wbf_ref, z_ref, lambda a, rows, cols: acts[1](a))


def _proj_rotary_kernel(h_ref, w_ref, cos_ref, sin_ref, z_ref, wbf_ref):
    _cast_weights_once([(w_ref, wbf_ref)])
    scale = jnp.where(pl.program_id(0) == 0, np.float32(RET_DK ** -0.5), np.float32(1.0))

    def rotary(a, rows, cols):
        cos, sin = cos_ref[rows, :] * scale, sin_ref[rows, :] * scale
        heads = []
        for hh in range(MXU_TILE // RET_DK):
            t = a[:, hh * RET_DK:(hh + 1) * RET_DK]
            heads.append(t * cos + pltpu.roll(t, RET_DK // 2, axis=1) * sin)
        return jnp.concatenate(heads, axis=1)

    _proj_units(h_ref, wbf_ref, z_ref, rotary)


def _in_proj(kernel_fn, name, h, w_in, layer, col0, width, extra=(), extra_specs=()):
    n = h.shape[0]
    assert col0 % IN_BN == 0 and width % IN_BN == 0
    cb0 = col0 // IN_BN
    return pl.pallas_call(
        kernel_fn,
        grid=(width // IN_BN, n // IN_BM),
        in_specs=[
            pl.BlockSpec((IN_BM, D_MODEL), lambda j, m: (m, 0)),
            pl.BlockSpec((None, D_MODEL, IN_BN), lambda j, m: (layer, 0, cb0 + j)),
            *extra_specs,
        ],
        out_specs=pl.BlockSpec((IN_BM, IN_BN), lambda j, m: (m, j)),
        out_shape=jax.ShapeDtypeStruct((n, width), BF16),
        scratch_shapes=[pltpu.VMEM((D_MODEL, IN_BN), BF16)],
        compiler_params=_params("arbitrary", "arbitrary"),
        name=name,
    )(h, w_in, *extra)


def _in_proj_all(h, w_in, layer, cos2, sin2, seq):
    assert IN_BN == RET_QK
    pos_blocks = seq // IN_BM
    table = pl.BlockSpec((IN_BM, RET_DK), lambda j, m: (m % pos_blocks, 0))
    zqk = _in_proj(_proj_rotary_kernel, "in_proj_qk", h, w_in, layer, COL_QK, 2 * RET_QK,
                   (cos2, sin2), (table, table))
    zvg = _in_proj(functools.partial(_proj_act2_kernel, acts=(lambda a: a, _silu),
                                     split=RET_V // IN_BN),
                   "in_proj_vg", h, w_in, layer, COL_V, 2 * RET_V)
    zsg = _in_proj(functools.partial(_proj_act2_kernel, acts=(_gelu_x2, _sigmoid),
                                     split=2 * D_MODEL // IN_BN),
                   "in_proj_sg", h, w_in, layer, COL_S, 4 * D_MODEL)
    return zqk, zvg, zsg


def _retention_kernel(q_ref, k_ref, v_ref, g_ref, dmask_ref, qdec_ref, kdec_ref, cdec_ref,
                      o_ref, state_ref, acc_ref):
    @pl.when(pl.program_id(2) == 0)
    def _():
        state_ref[...] = jnp.zeros_like(state_ref)

    heads = range(RET_HB)
    qk_cols = [slice(hh * RET_DK, (hh + 1) * RET_DK) for hh in heads]
    v_cols = [slice(hh * RET_DV, (hh + 1) * RET_DV) for hh in heads]
    contract_last = (((1,), (1,)), ((), ()))
    contract_first = (((0,), (0,)), ((), ()))

    def chunk_rows(t):
        return pl.ds(pl.multiple_of(t * RET_T, RET_T), RET_T)

    def matmul_stage(t):
        rows = chunk_rows(t)
        q = [q_ref[rows, c] for c in qk_cols]
        k = [k_ref[rows, c] for c in qk_cols]
        v = [v_ref[rows, c] for c in v_cols]
        s = [lax.dot_general(q[hh], k[hh], contract_last, preferred_element_type=F32)
             for hh in heads]
        st = [state_ref[hh] for hh in heads]
        kd = [(k[hh].astype(F32) * kdec_ref[hh]).astype(BF16) for hh in heads]
        upd = [lax.dot_general(kd[hh], v[hh], contract_first, preferred_element_type=F32)
               for hh in heads]
        for hh in heads:
            state_ref[hh] = st[hh] * cdec_ref[hh] + upd[hh]
        p = [(s[hh] * dmask_ref[hh]).astype(BF16) for hh in heads]
        qs = [(q[hh].astype(F32) * qdec_ref[hh]).astype(BF16) for hh in heads]
        for hh in heads:
            acc_ref[hh] = (jnp.dot(p[hh], v[hh], preferred_element_type=F32)
                           + jnp.dot(qs[hh], st[hh].astype(BF16), preferred_element_type=F32))

    def norm_stage(t):
        rows = chunk_rows(t)
        for hh in heads:
            o = acc_ref[hh]
            mu = jnp.mean(o, axis=-1, keepdims=True)
            d = o - mu
            var = jnp.mean(d * d, axis=-1, keepdims=True)
            y = d * lax.rsqrt(var + EPS) * g_ref[rows, v_cols[hh]].astype(F32)
            o_ref[rows, v_cols[hh]] = y.astype(o_ref.dtype)

    n_chunks = RET_TS // RET_T
    matmul_stage(0)

    def body(t, carry):
        norm_stage(t - 1)
        matmul_stage(t)
        return carry

    lax.fori_loop(1, n_chunks, body, 0)
    norm_stage(n_chunks - 1)


def _retention(zqk, zvg, consts, batch, seq):
    dmask, qdec, kdec, cdec = consts
    n = zqk.shape[0]
    sb = seq // RET_TS
    hg = RET_HEADS // RET_HB
    qk_w, v_w = RET_HB * RET_DK, RET_HB * RET_DV
    head_const = lambda r, c: pl.BlockSpec((RET_HB, r, c), lambda b, h, s: (h, 0, 0))
    v_spec = pl.BlockSpec((RET_TS, v_w), lambda b, h, s: (b * sb + s, h))
    return pl.pallas_call(
        _retention_kernel,
        grid=(batch, hg, sb),
        in_specs=[
            pl.BlockSpec((RET_TS, qk_w), lambda b, h, s: (b * sb + s, h)),
            pl.BlockSpec((RET_TS, qk_w), lambda b, h, s: (b * sb + s, hg + h)),
            v_spec,
            pl.BlockSpec((RET_TS, v_w), lambda b, h, s: (b * sb + s, hg + h)),
            head_const(RET_T, RET_T),
            head_const(RET_T, RET_DK),
            head_const(RET_T, RET_DK),
            head_const(RET_DK, RET_DV),
        ],
        out_specs=v_spec,
        out_shape=jax.ShapeDtypeStruct((n, RET_V), BF16),
        scratch_shapes=[pltpu.VMEM((RET_HB, RET_DK, RET_DV), F32),
                        pltpu.VMEM((RET_HB, RET_T, RET_DV), F32)],
        compiler_params=_params("parallel", "parallel", "arbitrary"),
        name="retention",
    )(zqk, zqk, zvg, zvg, dmask, qdec, kdec, cdec)


def _retention_consts():
    log_g = jnp.log1p(-(2.0 ** (-5.0 - jnp.arange(RET_HEADS, dtype=F32))))
    idx = jnp.arange(RET_T, dtype=F32)
    dist = jnp.abs(idx[:, None] - idx[None, :])
    chunk = jnp.arange(RET_T) // CHUNK
    visible = chunk[None, :] <= chunk[:, None]
    dmask = jnp.where(visible[None], jnp.exp(log_g[:, None, None] * dist), 0.0)
    qdec = jnp.exp(log_g[:, None] * (idx[None, :] + 1.0))
    kdec = jnp.exp(log_g[:, None] * (RET_T - 1.0 - idx[None, :]))
    cdec = jnp.exp(log_g * RET_T)
    qdec = jnp.broadcast_to(qdec[:, :, None], (RET_HEADS, RET_T, RET_DK))
    kdec = jnp.broadcast_to(kdec[:, :, None], (RET_HEADS, RET_T, RET_DK))
    cdec = jnp.broadcast_to(cdec[:, None, None], (RET_HEADS, RET_DK, RET_DV))
    return dmask.astype(F32), qdec.astype(F32), kdec.astype(F32), cdec.astype(F32)


def _sgu_kernel(zu_ref, zv_ref, lnw_ref, lnb_ref, ws_ref, bs_ref, o_ref):
    zv = zv_ref[...].astype(F32)
    mu = jnp.mean(zv, axis=-1, keepdims=True)
    d = zv - mu
    var = jnp.mean(d * d, axis=-1, keepdims=True)
    vn = (d * lax.rsqrt(var + 4.0 * EPS) * lnw_ref[...] + lnb_ref[...]).astype(BF16)
    row_chunk = lax.broadcasted_iota(jnp.int32, (SGU_LEN, SGU_LEN), 0) // CHUNK
    col_chunk = lax.broadcasted_iota(jnp.int32, (SGU_LEN, SGU_LEN), 1) // CHUNK
    visible = col_chunk <= row_chunk
    for g in range(SGU_GROUPS):
        wm = jnp.where(visible, 0.5 * ws_ref[g], 0.0).astype(BF16)
        cols = slice(g * SGU_CH, (g + 1) * SGU_CH)
        for rows in _tile_slices(SGU_BM, SGU_LEN):
            mixed = jnp.dot(wm, vn[rows, cols], preferred_element_type=F32) + bs_ref[g]
            o_ref[rows, cols] = (zu_ref[rows, cols].astype(F32) * mixed).astype(o_ref.dtype)


def _sgu(zsg, ln_w, ln_b, w_s, b_s):
    n = zsg.shape[0]
    bs_b = jnp.broadcast_to(0.5 * b_s[:, :, None], (SGU_GROUPS, SGU_LEN, SGU_CH))
    return pl.pallas_call(
        _sgu_kernel,
        grid=(n // SGU_BM,),
        in_specs=[
            pl.BlockSpec((SGU_BM, D_MODEL), lambda i: (i, 0)),
            pl.BlockSpec((SGU_BM, D_MODEL), lambda i: (i, 1)),
            pl.BlockSpec((1, D_MODEL), lambda i: (0, 0)),
            pl.BlockSpec((1, D_MODEL), lambda i: (0, 0)),
            pl.BlockSpec((SGU_GROUPS, SGU_LEN, SGU_LEN), lambda i: (0, 0, 0)),
            pl.BlockSpec((SGU_GROUPS, SGU_LEN, SGU_CH), lambda i: (0, 0, 0)),
        ],
        out_specs=pl.BlockSpec((SGU_BM, D_MODEL), lambda i: (i, 0)),
        out_shape=jax.ShapeDtypeStruct((n, D_MODEL), BF16),
        compiler_params=_params("parallel"),
        name="spatial_gating",
    )(zsg, zsg, ln_w.reshape(1, D_MODEL), ln_b.reshape(1, D_MODEL), w_s, bs_b)


def _branch_merge_kernel(a_ref, b_ref, ga_ref, gb_ref, wa_ref, wb_ref, gain_ref, o_ref,
                         wabf_ref, wbbf_ref):
    @pl.when(pl.program_id(1) == 0)
    def _():
        wabf_ref[...] = (wa_ref[...] * gain_ref[...]).astype(BF16)
        wbbf_ref[...] = wb_ref[...].astype(BF16)

    for rows in _tile_slices(MG_BM):
        for cols in _tile_slices(MG_BN):
            pa = jnp.dot(a_ref[rows, :], wabf_ref[:, cols], preferred_element_type=F32)
            pb = jnp.dot(b_ref[rows, :], wbbf_ref[:, cols], preferred_element_type=F32)
            merged = ga_ref[rows, cols].astype(F32) * pa + gb_ref[rows, cols].astype(F32) * pb
            o_ref[rows, cols] = merged.astype(o_ref.dtype)


def _branch_merge(a, b, zsg, wa, wb, layer, a_gain):
    n = a.shape[0]
    nb = D_MODEL // MG_BN
    ga0 = 2 * nb
    act = pl.BlockSpec((MG_BM, D_MODEL), lambda j, m: (m, 0))
    weight = pl.BlockSpec((None, D_MODEL, MG_BN), lambda j, m: (layer, 0, j))
    return pl.pallas_call(
        _branch_merge_kernel,
        grid=(nb, n // MG_BM),
        in_specs=[act, act,
                  pl.BlockSpec((MG_BM, MG_BN), lambda j, m: (m, ga0 + j)),
                  pl.BlockSpec((MG_BM, MG_BN), lambda j, m: (m, ga0 + nb + j)),
                  weight, weight,
                  pl.BlockSpec((D_MODEL, 1), lambda j, m: (0, 0))],
        out_specs=pl.BlockSpec((MG_BM, MG_BN), lambda j, m: (m, j)),
        out_shape=jax.ShapeDtypeStruct((n, D_MODEL), BF16),
        scratch_shapes=[pltpu.VMEM((D_MODEL, MG_BN), BF16), pltpu.VMEM((D_MODEL, MG_BN), BF16)],
        compiler_params=_params("arbitrary", "arbitrary"),
        name="branch_merge",
    )(a, b, zsg, zsg, wa, wb, a_gain.reshape(RET_V, 1))


def _out_proj_kernel(m_ref, x_ref, w_ref, nw_ref, xo_ref, ho_ref, wbf_ref):
    @pl.when(pl.program_id(0) == 0)
    def _():
        wbf_ref[...] = w_ref[...].astype(BF16)

    for rows in _tile_slices(OUT_BM):
        x = x_ref[rows, :] + jnp.dot(m_ref[rows, :], wbf_ref[...], preferred_element_type=F32)
        xo_ref[rows, :] = x
        ho_ref[rows, :] = _rms_norm_rows(x, nw_ref[...]).astype(ho_ref.dtype)


def _out_proj(merged, x, w_out, layer, norm_w):
    n = x.shape[0]
    row = pl.BlockSpec((OUT_BM, D_MODEL), lambda i: (i, 0))
    return pl.pallas_call(
        _out_proj_kernel,
        grid=(n // OUT_BM,),
        in_specs=[row, row,
                  _resident((None, D_MODEL, D_MODEL), lambda i: (layer, 0, 0)),
                  pl.BlockSpec((1, D_MODEL), lambda i: (0, 0))],
        out_specs=[row, row],
        out_shape=[jax.ShapeDtypeStruct((n, D_MODEL), F32),
                   jax.ShapeDtypeStruct((n, D_MODEL), BF16)],
        scratch_shapes=[pltpu.VMEM((D_MODEL, D_MODEL), BF16)],
        compiler_params=_params("arbitrary"),
        name="out_proj",
    )(merged, x, w_out, norm_w.reshape(1, D_MODEL))


def _ffn_in_kernel(h_ref, wa_ref, wc_ref, u_ref, wabf_ref, wcbf_ref):
    _cast_weights_once([(wa_ref, wabf_ref), (wc_ref, wcbf_ref)])
    for rows in _tile_slices(FFN_BM):
        for cols in _tile_slices(FFN_BN):
            a = jnp.dot(h_ref[rows, :], wabf_ref[:, cols], preferred_element_type=F32)
            g = jnp.dot(h_ref[rows, :], wcbf_ref[:, cols], preferred_element_type=F32)
            u_ref[rows, cols] = (_silu(a) * g).astype(u_ref.dtype)


def _ffn_in(h, w_ffn_in, layer):
    n = h.shape[0]
    nb = D_FF // FFN_BN
    return pl.pallas_call(
        _ffn_in_kernel,
        grid=(nb, n // FFN_BM),
        in_specs=[
            pl.BlockSpec((FFN_BM, D_MODEL), lambda j, m: (m, 0)),
            pl.BlockSpec((None, D_MODEL, FFN_BN), lambda j, m: (layer, 0, j)),
            pl.BlockSpec((None, D_MODEL, FFN_BN), lambda j, m: (layer, 0, nb + j)),
        ],
        out_specs=pl.BlockSpec((FFN_BM, FFN_BN), lambda j, m: (m, j)),
        out_shape=jax.ShapeDtypeStruct((n, D_FF), BF16),
        scratch_shapes=[pltpu.VMEM((D_MODEL, FFN_BN), BF16),
                        pltpu.VMEM((D_MODEL, FFN_BN), BF16)],
        compiler_params=_params("arbitrary", "arbitrary"),
        name="ffn_in",
    )(h, w_ffn_in, w_ffn_in)


def _ffn_out_kernel(u_ref, x_ref, w_ref, nw_ref, xo_ref, ho_ref):
    x = x_ref[...] + jnp.dot(u_ref[...], w_ref[...], preferred_element_type=F32)
    xo_ref[...] = x
    ho_ref[...] = _rms_norm_rows(x, nw_ref[...]).astype(ho_ref.dtype)


def _ffn_out(u, x, w, layer, norm_w, h_dtype):
    n = x.shape[0]
    row = lambda width: pl.BlockSpec((ROW_BM, width), lambda i: (i, 0))
    return pl.pallas_call(
        _ffn_out_kernel,
        grid=(n // ROW_BM,),
        in_specs=[row(D_FF), row(D_MODEL),
                  _resident((D_FF, D_MODEL), lambda i: (layer, 0)),
                  pl.BlockSpec((1, D_MODEL), lambda i: (0, 0))],
        out_specs=[row(D_MODEL), row(D_MODEL)],
        out_shape=[jax.ShapeDtypeStruct((n, D_MODEL), F32),
                   jax.ShapeDtypeStruct((n, D_MODEL), h_dtype)],
        compiler_params=_params("parallel"),
        name="ffn_out",
    )(u, x, w, norm_w.reshape(1, D_MODEL))


def _rotary_tables(seq):
    half = RET_DK // 2
    inv = ROPE_BASE ** (-jnp.arange(half, dtype=F32) / half)
    ang = jnp.arange(seq, dtype=jnp.int32).astype(F32)[:, None] * inv[None, :]
    cos, sin = jnp.cos(ang), jnp.sin(ang)
    return jnp.concatenate([cos, cos], axis=1), jnp.concatenate([-sin, sin], axis=1)


def kernel(x, norm_mix_w, w_in, ret_gn_w, ret_proj, sgu_ln_w, sgu_ln_b, sgu_w_s, sgu_b_s,
           sgu_proj, w_out, norm_ffn_w, w_ffn_in, w_ffn_out, final_norm_w):
    batch, seq, d = x.shape
    assert d == D_MODEL and seq % IN_BM == 0 and seq % RET_TS == 0 and RET_TS % RET_T == 0
    n = batch * seq
    cos2, sin2 = _rotary_tables(seq)
    ret_consts = _retention_consts()
    w_ffn_out_bf = _cast_bf16(w_ffn_out)

    xf = x.reshape(n, d)
    h = _rms_norm(xf, norm_mix_w[0], BF16)
    for l in range(DEPTH):
        zqk, zvg, zsg = _in_proj_all(h, w_in, l, cos2, sin2, seq)
        ret = _retention(zqk, zvg, ret_consts, batch, seq)
        sgu = _sgu(zsg, sgu_ln_w[l], sgu_ln_b[l], sgu_w_s[l], sgu_b_s[l])
        merged = _branch_merge(ret, sgu, zsg, ret_proj, sgu_proj, l, ret_gn_w[l])
        xf, h = _out_proj(merged, xf, w_out, l, norm_ffn_w[l])
        u = _ffn_in(h, w_ffn_in, l)
        last = l == DEPTH - 1
        next_w = final_norm_w if last else norm_mix_w[l + 1]
        xf, h = _ffn_out(u, xf, w_ffn_out_bf, l, next_w, F32 if last else BF16)
    return h.reshape(batch, seq, d)
```

```python
import functools

import jax
import jax.numpy as jnp
import numpy as np
from jax import lax
from jax.experimental import pallas as pl
from jax.experimental.pallas import tpu as pltpu

D_MODEL = 2048
DEPTH = 4
CHUNK = 64
RET_HEADS = 8
RET_DK = 128
RET_DV = 256
RET_QK = RET_HEADS * RET_DK
RET_V = RET_HEADS * RET_DV
SGU_GROUPS = 8
SGU_LEN = 128
SGU_CH = D_MODEL // SGU_GROUPS
D_FF = 5632
ROPE_BASE = 10000.0
EPS = 1e-6

COL_QK = 0
COL_V = 2 * RET_QK
COL_G = COL_V + RET_V
COL_S = COL_G + RET_V
COL_GATE = COL_S + 2 * D_MODEL

F32 = jnp.float32
BF16 = jnp.bfloat16

V7X_VMEM_BYTES = 64 * 2**20
VMEM_LIMIT_BYTES = V7X_VMEM_BYTES - 8 * 2**20
MXU_TILE = 256

IN_BM, IN_BN = 2048, 1024
FFN_BM, FFN_BN = 2048, 512
MG_BM, MG_BN = 1024, 512
OUT_BM = 512
ROW_BM = 256
RET_T = 256
RET_TS = 2048
RET_HB = 4
SGU_BM = 512
CAST_BM = 1024


def _params(*sem):
    return pltpu.CompilerParams(dimension_semantics=sem, vmem_limit_bytes=VMEM_LIMIT_BYTES)


def _sigmoid(x):
    return 1.0 / (1.0 + jnp.exp2(x * np.float32(-np.log2(np.e))))


def _silu(x):
    return x * _sigmoid(x)


def _gelu_x2(x):
    return x * (1.0 + lax.erf(x * np.float32(1.0 / np.sqrt(2.0))))


def _rms_norm_rows(x, w):
    return x * lax.rsqrt(jnp.mean(x * x, axis=-1, keepdims=True) + EPS) * w


def _resident(shape, index_map):
    return pl.BlockSpec(shape, index_map, pipeline_mode=pl.Buffered(1))


def _tile_slices(n, size=MXU_TILE):
    return [slice(i * size, (i + 1) * size) for i in range(n // size)]


def _norm_kernel(x_ref, w_ref, o_ref):
    o_ref[...] = _rms_norm_rows(x_ref[...], w_ref[...]).astype(o_ref.dtype)


def _rms_norm(x, w, out_dtype):
    n, d = x.shape
    return pl.pallas_call(
        _norm_kernel,
        grid=(n // CAST_BM,),
        in_specs=[pl.BlockSpec((CAST_BM, d), lambda i: (i, 0)),
                  pl.BlockSpec((1, d), lambda i: (0, 0))],
        out_specs=pl.BlockSpec((CAST_BM, d), lambda i: (i, 0)),
        out_shape=jax.ShapeDtypeStruct((n, d), out_dtype),
        compiler_params=_params("parallel"),
        name="rms_norm",
    )(x, w.reshape(1, d))


def _cast_weights_once(pairs):
    @pl.when(pl.program_id(1) == 0)
    def _():
        for w_ref, wbf_ref in pairs:
            wbf_ref[...] = w_ref[...].astype(BF16)


def _proj_units(h_ref, wbf_ref, z_ref, epilogue):
    for rows in _tile_slices(z_ref.shape[0]):
        for cols in _tile_slices(z_ref.shape[1]):
            a = jnp.dot(h_ref[rows, :], wbf_ref[:, cols], preferred_element_type=F32)
            z_ref[rows, cols] = epilogue(a, rows, cols).astype(z_ref.dtype)


def _proj_act_kernel(h_ref, w_ref, z_ref, wbf_ref, *, act):
    _cast_weights_once([(w_ref, wbf_ref)])
    _proj_units(h_ref, wbf_ref, z_ref, lambda a, rows, cols: act(a))


def _proj_act2_kernel(h_ref, w_ref, z_ref, wbf_ref, *, acts, split):
    _cast_weights_once([(w_ref, wbf_ref)])

    @pl.when(pl.program_id(0) < split)
    def _():
        _proj_units(h_ref, wbf_ref, z_ref, lambda a, rows, cols: acts[0](a))

    @pl.when(pl.program_id(0) >= split)
    def _():
        _proj_units(h_ref, wbf_ref, z_ref, lambda a, rows, cols: acts[1](a))


def _proj_rotary_kernel(h_ref, w_ref, cos_ref, sin_ref, z_ref, wbf_ref):
    _cast_weights_once([(w_ref, wbf_ref)])

    def rotary(a, rows, cols):
        cos, sin = cos_ref[rows, :], sin_ref[rows, :]
        heads = []
        for hh in range(MXU_TILE // RET_DK):
            t = a[:, hh * RET_DK:(hh + 1) * RET_DK]
            heads.append(t * cos + pltpu.roll(t, RET_DK // 2, axis=1) * sin)
        return jnp.concatenate(heads, axis=1)

    _proj_units(h_ref, wbf_ref, z_ref, rotary)


def _in_proj(kernel_fn, name, h, w_in, layer, col0, width, bn, extra=(), extra_specs=()):
    n = h.shape[0]
    assert col0 % bn == 0 and width % bn == 0
    cb0 = col0 // bn
    return pl.pallas_call(
        kernel_fn,
        grid=(width // bn, n // IN_BM),
        in_specs=[
            pl.BlockSpec((IN_BM, D_MODEL), lambda j, m: (m, 0)),
            pl.BlockSpec((None, D_MODEL, bn), lambda j, m: (layer, 0, cb0 + j)),
            *extra_specs,
        ],
        out_specs=pl.BlockSpec((None, IN_BM, bn), lambda j, m: (j, m, 0)),
        out_shape=jax.ShapeDtypeStruct((width // bn, n, bn), BF16),
        scratch_shapes=[pltpu.VMEM((D_MODEL, bn), BF16)],
        compiler_params=_params("arbitrary", "arbitrary"),
        name=name,
    )(h, w_in, *extra)


def _in_proj_all(h, w_in, layer, cos2, sin2, seq):
    pos_blocks = seq // IN_BM
    qk_bn = RET_HB * RET_DK
    slabs_per_side = RET_QK // qk_bn
    table = pl.BlockSpec((None, IN_BM, RET_DK),
                         lambda j, m: (j // slabs_per_side, m % pos_blocks, 0))
    zqk = _in_proj(_proj_rotary_kernel, "in_proj_qk", h, w_in, layer, COL_QK, 2 * RET_QK, qk_bn,
                   (cos2, sin2), (table, table))
    zvg = _in_proj(functools.partial(_proj_act2_kernel, acts=(lambda a: a, _silu),
                                     split=RET_V // IN_BN),
                   "in_proj_vg", h, w_in, layer, COL_V, 2 * RET_V, IN_BN)
    zsg = _in_proj(functools.partial(_proj_act2_kernel, acts=(_gelu_x2, _sigmoid),
                                     split=2 * D_MODEL // IN_BN),
                   "in_proj_sg", h, w_in, layer, COL_S, 4 * D_MODEL, IN_BN)
    return zqk, zvg, zsg


def _retention_kernel(q_ref, k_ref, v_ref, g_ref, dmask_ref, qdec_ref, kdec_ref, cdec_ref,
                      o_ref, state_ref, acc_ref):
    @pl.when(pl.program_id(2) == 0)
    def _():
        state_ref[...] = jnp.zeros_like(state_ref)

    heads = range(RET_HB)
    qk_cols = [slice(hh * RET_DK, (hh + 1) * RET_DK) for hh in heads]
    v_cols = [slice(hh * RET_DV, (hh + 1) * RET_DV) for hh in heads]
    contract_last = (((1,), (1,)), ((), ()))
    contract_first = (((0,), (0,)), ((), ()))

    def chunk_rows(t):
        return pl.ds(pl.multiple_of(t * RET_T, RET_T), RET_T)

    def matmul_stage(t):
        rows = chunk_rows(t)
        q = [q_ref[rows, c] for c in qk_cols]
        k = [k_ref[rows, c] for c in qk_cols]
        v = [v_ref[rows, c] for c in v_cols]
        s = [lax.dot_general(q[hh], k[hh], contract_last, preferred_element_type=F32)
             for hh in heads]
        st = [state_ref[hh] for hh in heads]
        kd = [(k[hh].astype(F32) * kdec_ref[hh]).astype(BF16) for hh in heads]
        upd = [lax.dot_general(kd[hh], v[hh], contract_first, preferred_element_type=F32)
               for hh in heads]
        for hh in heads:
            state_ref[hh] = st[hh] * cdec_ref[hh] + upd[hh]
        p = [(s[hh] * dmask_ref[hh]).astype(BF16) for hh in heads]
        qs = [(q[hh].astype(F32) * qdec_ref[hh]).astype(BF16) for hh in heads]
        for hh in heads:
            acc_ref[hh] = (jnp.dot(p[hh], v[hh], preferred_element_type=F32)
                           + jnp.dot(qs[hh], st[hh].astype(BF16), preferred_element_type=F32))

    def norm_stage(t):
        rows = chunk_rows(t)
        for hh in heads:
            o = acc_ref[hh]
            mu = jnp.mean(o, axis=-1, keepdims=True)
            d = o - mu
            var = jnp.mean(d * d, axis=-1, keepdims=True)
            y = d * lax.rsqrt(var + EPS) * g_ref[rows, v_cols[hh]].astype(F32)
            o_ref[rows, v_cols[hh]] = y.astype(o_ref.dtype)

    n_chunks = RET_TS // RET_T
    matmul_stage(0)

    def body(t, carry):
        norm_stage(t - 1)
        matmul_stage(t)
        return carry

    lax.fori_loop(1, n_chunks, body, 0)
    norm_stage(n_chunks - 1)


def _retention(zqk, zvg, consts, batch, seq):
    dmask, qdec, kdec, cdec = consts
    n = zqk.shape[1]
    sb = seq // RET_TS
    hg = RET_HEADS // RET_HB
    qk_w, v_w = RET_HB * RET_DK, RET_HB * RET_DV
    assert zqk.shape == (2 * hg, n, qk_w) and zvg.shape == (2 * hg, n, v_w)
    head_const = lambda r, c: pl.BlockSpec((RET_HB, r, c), lambda b, h, s: (h, 0, 0))
    slab = lambda w, first: pl.BlockSpec((None, RET_TS, w),
                                         lambda b, h, s: (first + h, b * sb + s, 0))
    return pl.pallas_call(
        _retention_kernel,
        grid=(batch, hg, sb),
        in_specs=[
            slab(qk_w, 0),
            slab(qk_w, hg),
            slab(v_w, 0),
            slab(v_w, hg),
            head_const(RET_T, RET_T),
            head_const(RET_T, RET_DK),
            head_const(RET_T, RET_DK),
            head_const(RET_DK, RET_DV),
        ],
        out_specs=slab(v_w, 0),
        out_shape=jax.ShapeDtypeStruct((hg, n, v_w), BF16),
        scratch_shapes=[pltpu.VMEM((RET_HB, RET_DK, RET_DV), F32),
                        pltpu.VMEM((RET_HB, RET_T, RET_DV), F32)],
        compiler_params=_params("parallel", "parallel", "arbitrary"),
        name="retention",
    )(zqk, zqk, zvg, zvg, dmask, qdec, kdec, cdec)


def _retention_consts():
    log_g = jnp.log1p(-(2.0 ** (-5.0 - jnp.arange(RET_HEADS, dtype=F32))))
    idx = jnp.arange(RET_T, dtype=F32)
    dist = jnp.abs(idx[:, None] - idx[None, :])
    chunk = jnp.arange(RET_T) // CHUNK
    visible = chunk[None, :] <= chunk[:, None]
    dmask = jnp.where(visible[None], jnp.exp(log_g[:, None, None] * dist), 0.0)
    qdec = jnp.exp(log_g[:, None] * (idx[None, :] + 1.0))
    kdec = jnp.exp(log_g[:, None] * (RET_T - 1.0 - idx[None, :]))
    cdec = jnp.exp(log_g * RET_T)
    qdec = jnp.broadcast_to(qdec[:, :, None], (RET_HEADS, RET_T, RET_DK))
    kdec = jnp.broadcast_to(kdec[:, :, None], (RET_HEADS, RET_T, RET_DK))
    cdec = jnp.broadcast_to(cdec[:, None, None], (RET_HEADS, RET_DK, RET_DV))
    return dmask.astype(F32), qdec.astype(F32), kdec.astype(F32), cdec.astype(F32)


def _sgu_kernel(zu_ref, zv_ref, lnw_ref, lnb_ref, ws_ref, bs_ref, o_ref):
    zv = jnp.concatenate([zv_ref[i] for i in range(zv_ref.shape[0])], axis=1).astype(F32)
    mu = jnp.mean(zv, axis=-1, keepdims=True)
    d = zv - mu
    var = jnp.mean(d * d, axis=-1, keepdims=True)
    vn = (d * lax.rsqrt(var + 4.0 * EPS) * lnw_ref[...] + lnb_ref[...]).astype(BF16)
    row_chunk = lax.broadcasted_iota(jnp.int32, (SGU_LEN, SGU_LEN), 0) // CHUNK
    col_chunk = lax.broadcasted_iota(jnp.int32, (SGU_LEN, SGU_LEN), 1) // CHUNK
    visible = col_chunk <= row_chunk
    for g in range(SGU_GROUPS):
        wm = jnp.where(visible, 0.5 * ws_ref[g], 0.0).astype(BF16)
        cols = slice(g * SGU_CH, (g + 1) * SGU_CH)
        slab = g * SGU_CH // IN_BN
        slab_cols = slice(g * SGU_CH - slab * IN_BN, (g + 1) * SGU_CH - slab * IN_BN)
        for rows in _tile_slices(SGU_BM, SGU_LEN):
            mixed = jnp.dot(wm, vn[rows, cols], preferred_element_type=F32) + bs_ref[g]
            o_ref[slab, rows, slab_cols] = zu_ref[slab, rows, slab_cols] * mixed.astype(BF16)


def _sgu(zsg, ln_w, ln_b, w_s, b_s):
    n = zsg.shape[1]
    slabs = D_MODEL // IN_BN
    bs_b = jnp.broadcast_to(0.5 * b_s[:, :, None], (SGU_GROUPS, SGU_LEN, SGU_CH))
    rows = lambda first: pl.BlockSpec((slabs, SGU_BM, IN_BN), lambda i: (first, i, 0))
    return pl.pallas_call(
        _sgu_kernel,
        grid=(n // SGU_BM,),
        in_specs=[
            rows(0),
            rows(1),
            pl.BlockSpec((1, D_MODEL), lambda i: (0, 0)),
            pl.BlockSpec((1, D_MODEL), lambda i: (0, 0)),
            pl.BlockSpec((SGU_GROUPS, SGU_LEN, SGU_LEN), lambda i: (0, 0, 0)),
            pl.BlockSpec((SGU_GROUPS, SGU_LEN, SGU_CH), lambda i: (0, 0, 0)),
        ],
        out_specs=rows(0),
        out_shape=jax.ShapeDtypeStruct((slabs, n, IN_BN), BF16),
        compiler_params=_params("parallel"),
        name="spatial_gating",
    )(zsg, zsg, ln_w.reshape(1, D_MODEL), ln_b.reshape(1, D_MODEL), w_s, bs_b)


def _branch_merge_kernel(a_ref, b_ref, ga_ref, gb_ref, wa_ref, wb_ref, gain_ref, o_ref,
                         wabf_ref, wbbf_ref):
    @pl.when(pl.program_id(1) == 0)
    def _():
        wabf_ref[...] = (wa_ref[...] * gain_ref[...]).astype(BF16)
        wbbf_ref[...] = wb_ref[...].astype(BF16)

    def slab_dot(x_ref, w_ref, rows, cols):
        width = x_ref.shape[2]
        acc = None
        for i in range(x_ref.shape[0]):
            part = jnp.dot(x_ref[i, rows, :], w_ref[i * width:(i + 1) * width, cols],
                           preferred_element_type=F32)
            acc = part if acc is None else acc + part
        return acc

    for rows in _tile_slices(MG_BM):
        for cols in _tile_slices(MG_BN):
            pa = slab_dot(a_ref, wabf_ref, rows, cols)
            pb = slab_dot(b_ref, wbbf_ref, rows, cols)
            merged = ga_ref[rows, cols].astype(F32) * pa + gb_ref[rows, cols].astype(F32) * pb
            o_ref[rows, cols] = merged.astype(o_ref.dtype)


def _branch_merge(a, b, zsg, wa, wb, layer, a_gain):
    n = a.shape[1]
    nb = D_MODEL // MG_BN
    per_slab = IN_BN // MG_BN
    ga0 = 2 * nb
    act = lambda x: pl.BlockSpec((x.shape[0], MG_BM, x.shape[2]), lambda j, m: (0, m, 0))
    gate = lambda first: pl.BlockSpec(
        (None, MG_BM, MG_BN), lambda j, m: ((first + j) // per_slab, m, (first + j) % per_slab))
    weight = pl.BlockSpec((None, D_MODEL, MG_BN), lambda j, m: (layer, 0, j))
    return pl.pallas_call(
        _branch_merge_kernel,
        grid=(nb, n // MG_BM),
        in_specs=[act(a), act(b),
                  gate(ga0),
                  gate(ga0 + nb),
                  weight, weight,
                  pl.BlockSpec((D_MODEL, 1), lambda j, m: (0, 0))],
        out_specs=pl.BlockSpec((MG_BM, MG_BN), lambda j, m: (m, j)),
        out_shape=jax.ShapeDtypeStruct((n, D_MODEL), BF16),
        scratch_shapes=[pltpu.VMEM((D_MODEL, MG_BN), BF16), pltpu.VMEM((D_MODEL, MG_BN), BF16)],
        compiler_params=_params("arbitrary", "arbitrary"),
        name="branch_merge",
    )(a, b, zsg, zsg, wa, wb, a_gain.reshape(RET_V, 1))


def _out_proj_kernel(m_ref, x_ref, w_ref, nw_ref, xo_ref, ho_ref, wbf_ref):
    @pl.when(pl.program_id(0) == 0)
    def _():
        wbf_ref[...] = w_ref[...].astype(BF16)

    for rows in _tile_slices(OUT_BM):
        x = x_ref[rows, :] + jnp.dot(m_ref[rows, :], wbf_ref[...], preferred_element_type=F32)
        xo_ref[rows, :] = x
        ho_ref[rows, :] = _rms_norm_rows(x, nw_ref[...]).astype(ho_ref.dtype)


def _out_proj(merged, x, w_out, layer, norm_w):
    n = x.shape[0]
    row = pl.BlockSpec((OUT_BM, D_MODEL), lambda i: (i, 0))
    return pl.pallas_call(
        _out_proj_kernel,
        grid=(n // OUT_BM,),
        in_specs=[row, row,
                  _resident((None, D_MODEL, D_MODEL), lambda i: (layer, 0, 0)),
                  pl.BlockSpec((1, D_MODEL), lambda i: (0, 0))],
        out_specs=[row, row],
        out_shape=[jax.ShapeDtypeStruct((n, D_MODEL), F32),
                   jax.ShapeDtypeStruct((n, D_MODEL), BF16)],
        scratch_shapes=[pltpu.VMEM((D_MODEL, D_MODEL), BF16)],
        compiler_params=_params("arbitrary"),
        name="out_proj",
    )(merged, x, w_out, norm_w.reshape(1, D_MODEL))


def _ffn_in_kernel(h_ref, wa_ref, wc_ref, wdown_ref, u_ref, wdown_bf_ref, wabf_ref, wcbf_ref):
    _cast_weights_once([(wa_ref, wabf_ref), (wc_ref, wcbf_ref), (wdown_ref, wdown_bf_ref)])
    for rows in _tile_slices(FFN_BM):
        for cols in _tile_slices(FFN_BN):
            a = jnp.dot(h_ref[rows, :], wabf_ref[:, cols], preferred_element_type=F32)
            g = jnp.dot(h_ref[rows, :], wcbf_ref[:, cols], preferred_element_type=F32)
            u_ref[rows, cols] = (_silu(a) * g).astype(u_ref.dtype)


def _ffn_in(h, w_ffn_in, w_ffn_out, layer):
    n = h.shape[0]
    nb = D_FF // FFN_BN
    return pl.pallas_call(
        _ffn_in_kernel,
        grid=(nb, n // FFN_BM),
        in_specs=[
            pl.BlockSpec((FFN_BM, D_MODEL), lambda j, m: (m, 0)),
            pl.BlockSpec((None, D_MODEL, FFN_BN), lambda j, m: (layer, 0, j)),
            pl.BlockSpec((None, D_MODEL, FFN_BN), lambda j, m: (layer, 0, nb + j)),
            pl.BlockSpec((None, FFN_BN, D_MODEL), lambda j, m: (layer, j, 0)),
        ],
        out_specs=[pl.BlockSpec((FFN_BM, FFN_BN), lambda j, m: (m, j)),
                   pl.BlockSpec((FFN_BN, D_MODEL), lambda j, m: (j, 0))],
        out_shape=[jax.ShapeDtypeStruct((n, D_FF), BF16),
                   jax.ShapeDtypeStruct((D_FF, D_MODEL), BF16)],
        scratch_shapes=[pltpu.VMEM((D_MODEL, FFN_BN), BF16),
                        pltpu.VMEM((D_MODEL, FFN_BN), BF16)],
        compiler_params=_params("arbitrary", "arbitrary"),
        name="ffn_in",
    )(h, w_ffn_in, w_ffn_in, w_ffn_out)


def _ffn_out_kernel(u_ref, x_ref, w_ref, nw_ref, xo_ref, ho_ref):
    x = x_ref[...] + jnp.dot(u_ref[...], w_ref[...], preferred_element_type=F32)
    xo_ref[...] = x
    ho_ref[...] = _rms_norm_rows(x, nw_ref[...]).astype(ho_ref.dtype)


def _ffn_out(u, x, w_bf, norm_w, h_dtype):
    n = x.shape[0]
    row = lambda width: pl.BlockSpec((ROW_BM, width), lambda i: (i, 0))
    return pl.pallas_call(
        _ffn_out_kernel,
        grid=(n // ROW_BM,),
        in_specs=[row(D_FF), row(D_MODEL),
                  _resident((D_FF, D_MODEL), lambda i: (0, 0)),
                  pl.BlockSpec((1, D_MODEL), lambda i: (0, 0))],
        out_specs=[row(D_MODEL), row(D_MODEL)],
        out_shape=[jax.ShapeDtypeStruct((n, D_MODEL), F32),
                   jax.ShapeDtypeStruct((n, D_MODEL), h_dtype)],
        compiler_params=_params("parallel"),
        name="ffn_out",
    )(u, x, w_bf, norm_w.reshape(1, D_MODEL))


def _rotary_tables(seq):
    half = RET_DK // 2
    inv = ROPE_BASE ** (-jnp.arange(half, dtype=F32) / half)
    ang = jnp.arange(seq, dtype=jnp.int32).astype(F32)[:, None] * inv[None, :]
    cos, sin = jnp.cos(ang), jnp.sin(ang)
    cos2, sin2 = jnp.concatenate([cos, cos], axis=1), jnp.concatenate([-sin, sin], axis=1)
    scale = jnp.asarray([RET_DK ** -0.5, 1.0], F32)[:, None, None]
    return cos2[None] * scale, sin2[None] * scale


def kernel(x, norm_mix_w, w_in, ret_gn_w, ret_proj, sgu_ln_w, sgu_ln_b, sgu_w_s, sgu_b_s,
           sgu_proj, w_out, norm_ffn_w, w_ffn_in, w_ffn_out, final_norm_w):
    batch, seq, d = x.shape
    assert d == D_MODEL and seq % IN_BM == 0 and seq % RET_TS == 0 and RET_TS % RET_T == 0
    n = batch * seq
    cos2, sin2 = _rotary_tables(seq)
    ret_consts = _retention_consts()

    xf = x.reshape(n, d)
    h = _rms_norm(xf, norm_mix_w[0], BF16)
    for l in range(DEPTH):
        zqk, zvg, zsg = _in_proj_all(h, w_in, l, cos2, sin2, seq)
        ret = _retention(zqk, zvg, ret_consts, batch, seq)
        sgu = _sgu(zsg, sgu_ln_w[l], sgu_ln_b[l], sgu_w_s[l], sgu_b_s[l])
        merged = _branch_merge(ret, sgu, zsg, ret_proj, sgu_proj, l, ret_gn_w[l])
        xf, h = _out_proj(merged, xf, w_out, l, norm_ffn_w[l])
        u, w_down_bf = _ffn_in(h, w_ffn_in, w_ffn_out, l)
        last = l == DEPTH - 1
        next_w = final_norm_w if last else norm_mix_w[l + 1]
        xf, h = _ffn_out(u, xf, w_down_bf, next_w, F32 if last else BF16)
    return h.reshape(batch, seq, d)
```

```python
import functools

import jax
import jax.numpy as jnp
import numpy as np
from jax import lax
from jax.experimental import pallas as pl
from jax.experimental.pallas import tpu as pltpu

D_MODEL = 2048
DEPTH = 4
CHUNK = 64
RET_HEADS = 8
RET_DK = 128
RET_DV = 256
RET_QK = RET_HEADS * RET_DK
RET_V = RET_HEADS * RET_DV
SGU_GROUPS = 8
SGU_LEN = 128
SGU_CH = D_MODEL // SGU_GROUPS
D_FF = 5632
ROPE_BASE = 10000.0
EPS = 1e-6

COL_QK = 0
COL_V = 2 * RET_QK
COL_G = COL_V + RET_V
COL_S = COL_G + RET_V
COL_GATE = COL_S + 2 * D_MODEL

F32 = jnp.float32
BF16 = jnp.bfloat16

V7X_VMEM_BYTES = 64 * 2**20
VMEM_LIMIT_BYTES = V7X_VMEM_BYTES - 8 * 2**20
MXU_TILE = 256

IN_BM, IN_BN = 2048, 1024
FFN_BM, FFN_BN = 2048, 512
MG_BM, MG_BN = 1024, 512
OUT_BM = 512
ROW_BM = 256
RET_T = 256
RET_TS = 2048
RET_HB = 4
SGU_BM = 512
NORM_BM = 1024


def _params(*sem):
    return pltpu.CompilerParams(dimension_semantics=sem, vmem_limit_bytes=VMEM_LIMIT_BYTES)


def _sigmoid(x):
    return 1.0 / (1.0 + jnp.exp2(x * np.float32(-np.log2(np.e))))


def _silu(x):
    return x * _sigmoid(x)


def _gelu_x2(x):
    return x * (1.0 + lax.erf(x * np.float32(1.0 / np.sqrt(2.0))))


def _rms_norm_rows(x, w):
    return x * lax.rsqrt(jnp.mean(x * x, axis=-1, keepdims=True) + EPS) * w


def _resident(shape, index_map):
    return pl.BlockSpec(shape, index_map, pipeline_mode=pl.Buffered(1))


def _tile_slices(n, size=MXU_TILE):
    return [slice(i * size, (i + 1) * size) for i in range(n // size)]


def _norm_kernel(x_ref, w_ref, o_ref):
    o_ref[...] = _rms_norm_rows(x_ref[...], w_ref[...]).astype(o_ref.dtype)


def _rms_norm(x, w, out_dtype):
    n, d = x.shape
    return pl.pallas_call(
        _norm_kernel,
        grid=(n // NORM_BM,),
        in_specs=[pl.BlockSpec((NORM_BM, d), lambda i: (i, 0)),
                  pl.BlockSpec((1, d), lambda i: (0, 0))],
        out_specs=pl.BlockSpec((NORM_BM, d), lambda i: (i, 0)),
        out_shape=jax.ShapeDtypeStruct((n, d), out_dtype),
        compiler_params=_params("parallel"),
        name="rms_norm",
    )(x, w.reshape(1, d))


def _cast_weights_once(pairs):
    @pl.when(pl.program_id(1) == 0)
    def _():
        for w_ref, wbf_ref in pairs:
            wbf_ref[...] = w_ref[...].astype(BF16)


def _proj_units(h_ref, wbf_ref, z_ref, epilogue):
    for rows in _tile_slices(z_ref.shape[0]):
        for cols in _tile_slices(z_ref.shape[1]):
            a = jnp.dot(h_ref[rows, :], wbf_ref[:, cols], preferred_element_type=F32)
            z_ref[rows, cols] = epilogue(a, rows, cols).astype(z_ref.dtype)


def _proj_act2_kernel(h_ref, w_ref, z_ref, wbf_ref, *, acts, split):
    _cast_weights_once([(w_ref, wbf_ref)])

    @pl.when(pl.program_id(0) < split)
    def _():
        _proj_units(h_ref, wbf_ref, z_ref, lambda a, rows, cols: acts[0](a))

    @pl.when(pl.program_id(0) >= split)
    def _():
        _proj_units(h_ref, wbf_ref, z_ref, lambda a, rows, cols: acts[1](a))


def _proj_rotary_kernel(h_ref, w_ref, cos_ref, sin_ref, z_ref, wbf_ref):
    _cast_weights_once([(w_ref, wbf_ref)])

    def rotary(a, rows, cols):
        cos, sin = cos_ref[rows, :], sin_ref[rows, :]
        heads = []
        for hh in range(MXU_TILE // RET_DK):
            t = a[:, hh * RET_DK:(hh + 1) * RET_DK]
            heads.append(t * cos + pltpu.roll(t, RET_DK // 2, axis=1) * sin)
        return jnp.concatenate(heads, axis=1)

    _proj_units(h_ref, wbf_ref, z_ref, rotary)


def _in_proj(kernel_fn, name, h, w_in, layer, col0, width, extra=(), extra_specs=()):
    n = h.shape[0]
    assert col0 % IN_BN == 0 and width % IN_BN == 0
    cb0 = col0 // IN_BN
    return pl.pallas_call(
        kernel_fn,
        grid=(width // IN_BN, n // IN_BM),
        in_specs=[
            pl.BlockSpec((IN_BM, D_MODEL), lambda j, m: (m, 0)),
            pl.BlockSpec((None, D_MODEL, IN_BN), lambda j, m: (layer, 0, cb0 + j)),
            *extra_specs,
        ],
        out_specs=pl.BlockSpec((IN_BM, IN_BN), lambda j, m: (m, j)),
        out_shape=jax.ShapeDtypeStruct((n, width), BF16),
        scratch_shapes=[pltpu.VMEM((D_MODEL, IN_BN), BF16)],
        compiler_params=_params("arbitrary", "arbitrary"),
        name=name,
    )(h, w_in, *extra)


def _in_proj_all(h, w_in, layer, cos2, sin2, seq):
    assert IN_BN == RET_QK
    pos_blocks = seq // IN_BM
    table = pl.BlockSpec((None, IN_BM, RET_DK), lambda j, m: (j, m % pos_blocks, 0))
    zqk = _in_proj(_proj_rotary_kernel, "in_proj_qk", h, w_in, layer, COL_QK, 2 * RET_QK,
                   (cos2, sin2), (table, table))
    zvg = _in_proj(functools.partial(_proj_act2_kernel, acts=(lambda a: a, _silu),
                                     split=RET_V // IN_BN),
                   "in_proj_vg", h, w_in, layer, COL_V, 2 * RET_V)
    zsg = _in_proj(functools.partial(_proj_act2_kernel, acts=(_gelu_x2, _sigmoid),
                                     split=2 * D_MODEL // IN_BN),
                   "in_proj_sg", h, w_in, layer, COL_S, 4 * D_MODEL)
    return zqk, zvg, zsg


def _retention_kernel(q_ref, k_ref, v_ref, g_ref, dmask_ref, qdec_ref, kdec_ref, cdec_ref,
                      o_ref, state_ref, acc_ref):
    @pl.when(pl.program_id(2) == 0)
    def _():
        state_ref[...] = jnp.zeros_like(state_ref)

    heads = range(RET_HB)
    qk_cols = [slice(hh * RET_DK, (hh + 1) * RET_DK) for hh in heads]
    v_cols = [slice(hh * RET_DV, (hh + 1) * RET_DV) for hh in heads]
    contract_last = (((1,), (1,)), ((), ()))
    contract_first = (((0,), (0,)), ((), ()))

    def chunk_rows(t):
        return pl.ds(pl.multiple_of(t * RET_T, RET_T), RET_T)

    def matmul_stage(t):
        rows = chunk_rows(t)
        q = [q_ref[rows, c] for c in qk_cols]
        k = [k_ref[rows, c] for c in qk_cols]
        v = [v_ref[rows, c] for c in v_cols]
        s = [lax.dot_general(q[hh], k[hh], contract_last, preferred_element_type=F32)
             for hh in heads]
        st = [state_ref[hh] for hh in heads]
        kd = [(k[hh].astype(F32) * kdec_ref[hh]).astype(BF16) for hh in heads]
        upd = [lax.dot_general(kd[hh], v[hh], contract_first, preferred_element_type=F32)
               for hh in heads]
        for hh in heads:
            state_ref[hh] = st[hh] * cdec_ref[hh] + upd[hh]
        p = [(s[hh] * dmask_ref[hh]).astype(BF16) for hh in heads]
        qs = [(q[hh].astype(F32) * qdec_ref[hh]).astype(BF16) for hh in heads]
        for hh in heads:
            acc_ref[hh] = (jnp.dot(p[hh], v[hh], preferred_element_type=F32)
                           + jnp.dot(qs[hh], st[hh].astype(BF16), preferred_element_type=F32))

    def norm_stage(t):
        rows = chunk_rows(t)
        for hh in heads:
            o = acc_ref[hh]
            mu = jnp.mean(o, axis=-1, keepdims=True)
            d = o - mu
            var = jnp.mean(d * d, axis=-1, keepdims=True)
            y = d * lax.rsqrt(var + EPS) * g_ref[rows, v_cols[hh]].astype(F32)
            o_ref[rows, v_cols[hh]] = y.astype(o_ref.dtype)

    n_chunks = RET_TS // RET_T
    matmul_stage(0)

    def body(t, carry):
        norm_stage(t - 1)
        matmul_stage(t)
        return carry

    lax.fori_loop(1, n_chunks, body, 0)
    norm_stage(n_chunks - 1)


def _retention(zqk, zvg, consts, batch, seq):
    dmask, qdec, kdec, cdec = consts
    n = zqk.shape[0]
    sb = seq // RET_TS
    hg = RET_HEADS // RET_HB
    qk_w, v_w = RET_HB * RET_DK, RET_HB * RET_DV
    head_const = lambda r, c: pl.BlockSpec((RET_HB, r, c), lambda b, h, s: (h, 0, 0))
    v_spec = pl.BlockSpec((RET_TS, v_w), lambda b, h, s: (b * sb + s, h))
    return pl.pallas_call(
        _retention_kernel,
        grid=(batch, hg, sb),
        in_specs=[
            pl.BlockSpec((RET_TS, qk_w), lambda b, h, s: (b * sb + s, h)),
            pl.BlockSpec((RET_TS, qk_w), lambda b, h, s: (b * sb + s, hg + h)),
            v_spec,
            pl.BlockSpec((RET_TS, v_w), lambda b, h, s: (b * sb + s, hg + h)),
            head_const(RET_T, RET_T),
            head_const(RET_T, RET_DK),
            head_const(RET_T, RET_DK),
            head_const(RET_DK, RET_DV),
        ],
        out_specs=v_spec,
        out_shape=jax.ShapeDtypeStruct((n, RET_V), BF16),
        scratch_shapes=[pltpu.VMEM((RET_HB, RET_DK, RET_DV), F32),
                        pltpu.VMEM((RET_HB, RET_T, RET_DV), F32)],
        compiler_params=_params("parallel", "parallel", "arbitrary"),
        name="retention",
    )(zqk, zqk, zvg, zvg, dmask, qdec, kdec, cdec)


def _retention_consts():
    log_g = jnp.log1p(-(2.0 ** (-5.0 - jnp.arange(RET_HEADS, dtype=F32))))
    idx = jnp.arange(RET_T, dtype=F32)
    dist = jnp.abs(idx[:, None] - idx[None, :])
    chunk = jnp.arange(RET_T) // CHUNK
    visible = chunk[None, :] <= chunk[:, None]
    dmask = jnp.where(visible[None], jnp.exp(log_g[:, None, None] * dist), 0.0)
    qdec = jnp.exp(log_g[:, None] * (idx[None, :] + 1.0))
    kdec = jnp.exp(log_g[:, None] * (RET_T - 1.0 - idx[None, :]))
    cdec = jnp.exp(log_g * RET_T)
    qdec = jnp.broadcast_to(qdec[:, :, None], (RET_HEADS, RET_T, RET_DK))
    kdec = jnp.broadcast_to(kdec[:, :, None], (RET_HEADS, RET_T, RET_DK))
    cdec = jnp.broadcast_to(cdec[:, None, None], (RET_HEADS, RET_DK, RET_DV))
    return dmask.astype(F32), qdec.astype(F32), kdec.astype(F32), cdec.astype(F32)


def _sgu_kernel(zu_ref, zv_ref, lnw_ref, lnb_ref, ws_ref, bs_ref, o_ref):
    zv = zv_ref[...].astype(F32)
    mu = jnp.mean(zv, axis=-1, keepdims=True)
    d = zv - mu
    var = jnp.mean(d * d, axis=-1, keepdims=True)
    vn = (d * lax.rsqrt(var + 4.0 * EPS) * lnw_ref[...] + lnb_ref[...]).astype(BF16)
    row_chunk = lax.broadcasted_iota(jnp.int32, (SGU_LEN, SGU_LEN), 0) // CHUNK
    col_chunk = lax.broadcasted_iota(jnp.int32, (SGU_LEN, SGU_LEN), 1) // CHUNK
    visible = col_chunk <= row_chunk
    for g in range(SGU_GROUPS):
        wm = jnp.where(visible, 0.5 * ws_ref[g], 0.0).astype(BF16)
        cols = slice(g * SGU_CH, (g + 1) * SGU_CH)
        for rows in _tile_slices(SGU_BM, SGU_LEN):
            mixed = jnp.dot(wm, vn[rows, cols], preferred_element_type=F32) + bs_ref[g]
            o_ref[rows, cols] = (zu_ref[rows, cols].astype(F32) * mixed).astype(o_ref.dtype)


def _sgu(zsg, ln_w, ln_b, w_s, b_s):
    n = zsg.shape[0]
    bs_b = jnp.broadcast_to(0.5 * b_s[:, :, None], (SGU_GROUPS, SGU_LEN, SGU_CH))
    return pl.pallas_call(
        _sgu_kernel,
        grid=(n // SGU_BM,),
        in_specs=[
            pl.BlockSpec((SGU_BM, D_MODEL), lambda i: (i, 0)),
            pl.BlockSpec((SGU_BM, D_MODEL), lambda i: (i, 1)),
            pl.BlockSpec((1, D_MODEL), lambda i: (0, 0)),
            pl.BlockSpec((1, D_MODEL), lambda i: (0, 0)),
            pl.BlockSpec((SGU_GROUPS, SGU_LEN, SGU_LEN), lambda i: (0, 0, 0)),
            pl.BlockSpec((SGU_GROUPS, SGU_LEN, SGU_CH), lambda i: (0, 0, 0)),
        ],
        out_specs=pl.BlockSpec((SGU_BM, D_MODEL), lambda i: (i, 0)),
        out_shape=jax.ShapeDtypeStruct((n, D_MODEL), BF16),
        compiler_params=_params("parallel"),
        name="spatial_gating",
    )(zsg, zsg, ln_w.reshape(1, D_MODEL), ln_b.reshape(1, D_MODEL), w_s, bs_b)


def _branch_merge_kernel(a_ref, b_ref, ga_ref, gb_ref, wa_ref, wb_ref, gain_ref, o_ref,
                         wabf_ref, wbbf_ref):
    @pl.when(pl.program_id(1) == 0)
    def _():
        wabf_ref[...] = (wa_ref[...] * gain_ref[...]).astype(BF16)
        wbbf_ref[...] = wb_ref[...].astype(BF16)

    for rows in _tile_slices(MG_BM):
        for cols in _tile_slices(MG_BN):
            pa = jnp.dot(a_ref[rows, :], wabf_ref[:, cols], preferred_element_type=F32)
            pb = jnp.dot(b_ref[rows, :], wbbf_ref[:, cols], preferred_element_type=F32)
            merged = ga_ref[rows, cols].astype(F32) * pa + gb_ref[rows, cols].astype(F32) * pb
            o_ref[rows, cols] = merged.astype(o_ref.dtype)


def _branch_merge(a, b, zsg, wa, wb, layer, a_gain):
    n = a.shape[0]
    nb = D_MODEL // MG_BN
    ga0 = 2 * nb
    act = pl.BlockSpec((MG_BM, D_MODEL), lambda j, m: (m, 0))
    weight = pl.BlockSpec((None, D_MODEL, MG_BN), lambda j, m: (layer, 0, j))
    return pl.pallas_call(
        _branch_merge_kernel,
        grid=(nb, n // MG_BM),
        in_specs=[act, act,
                  pl.BlockSpec((MG_BM, MG_BN), lambda j, m: (m, ga0 + j)),
                  pl.BlockSpec((MG_BM, MG_BN), lambda j, m: (m, ga0 + nb + j)),
                  weight, weight,
                  pl.BlockSpec((D_MODEL, 1), lambda j, m: (0, 0))],
        out_specs=pl.BlockSpec((MG_BM, MG_BN), lambda j, m: (m, j)),
        out_shape=jax.ShapeDtypeStruct((n, D_MODEL), BF16),
        scratch_shapes=[pltpu.VMEM((D_MODEL, MG_BN), BF16), pltpu.VMEM((D_MODEL, MG_BN), BF16)],
        compiler_params=_params("arbitrary", "arbitrary"),
        name="branch_merge",
    )(a, b, zsg, zsg, wa, wb, a_gain.reshape(RET_V, 1))


def _out_proj_kernel(m_ref, x_ref, w_ref, nw_ref, xo_ref, ho_ref, wbf_ref):
    @pl.when(pl.program_id(0) == 0)
    def _():
        wbf_ref[...] = w_ref[...].astype(BF16)

    for rows in _tile_slices(OUT_BM):
        x = x_ref[rows, :] + jnp.dot(m_ref[rows, :], wbf_ref[...], preferred_element_type=F32)
        xo_ref[rows, :] = x
        ho_ref[rows, :] = _rms_norm_rows(x, nw_ref[...]).astype(ho_ref.dtype)


def _out_proj(merged, x, w_out, layer, norm_w):
    n = x.shape[0]
    row = pl.BlockSpec((OUT_BM, D_MODEL), lambda i: (i, 0))
    return pl.pallas_call(
        _out_proj_kernel,
        grid=(n // OUT_BM,),
        in_specs=[row, row,
                  _resident((None, D_MODEL, D_MODEL), lambda i: (layer, 0, 0)),
                  pl.BlockSpec((1, D_MODEL), lambda i: (0, 0))],
        out_specs=[row, row],
        out_shape=[jax.ShapeDtypeStruct((n, D_MODEL), F32),
                   jax.ShapeDtypeStruct((n, D_MODEL), BF16)],
        scratch_shapes=[pltpu.VMEM((D_MODEL, D_MODEL), BF16)],
        compiler_params=_params("arbitrary"),
        name="out_proj",
    )(merged, x, w_out, norm_w.reshape(1, D_MODEL))


def _ffn_in_kernel(h_ref, wa_ref, wc_ref, wdown_ref, u_ref, wdown_bf_ref, wabf_ref, wcbf_ref):
    _cast_weights_once([(wa_ref, wabf_ref), (wc_ref, wcbf_ref), (wdown_ref, wdown_bf_ref)])
    for rows in _tile_slices(FFN_BM):
        for cols in _tile_slices(FFN_BN):
            a = jnp.dot(h_ref[rows, :], wabf_ref[:, cols], preferred_element_type=F32)
            g = jnp.dot(h_ref[rows, :], wcbf_ref[:, cols], preferred_element_type=F32)
            u_ref[rows, cols] = (_silu(a) * g).astype(u_ref.dtype)


def _ffn_in(h, w_ffn_in, w_ffn_out, layer):
    n = h.shape[0]
    nb = D_FF // FFN_BN
    return pl.pallas_call(
        _ffn_in_kernel,
        grid=(nb, n // FFN_BM),
        in_specs=[
            pl.BlockSpec((FFN_BM, D_MODEL), lambda j, m: (m, 0)),
            pl.BlockSpec((None, D_MODEL, FFN_BN), lambda j, m: (layer, 0, j)),
            pl.BlockSpec((None, D_MODEL, FFN_BN), lambda j, m: (layer, 0, nb + j)),
            pl.BlockSpec((None, FFN_BN, D_MODEL), lambda j, m: (layer, j, 0)),
        ],
        out_specs=[pl.BlockSpec((FFN_BM, FFN_BN), lambda j, m: (m, j)),
                   pl.BlockSpec((FFN_BN, D_MODEL), lambda j, m: (j, 0))],
        out_shape=[jax.ShapeDtypeStruct((n, D_FF), BF16),
                   jax.ShapeDtypeStruct((D_FF, D_MODEL), BF16)],
        scratch_shapes=[pltpu.VMEM((D_MODEL, FFN_BN), BF16),
                        pltpu.VMEM((D_MODEL, FFN_BN), BF16)],
        compiler_params=_params("arbitrary", "arbitrary"),
        name="ffn_in",
    )(h, w_ffn_in, w_ffn_in, w_ffn_out)


def _ffn_out_kernel(u_ref, x_ref, w_ref, nw_ref, *out_refs):
    x = x_ref[...] + jnp.dot(u_ref[...], w_ref[...], preferred_element_type=F32)
    for xo_ref in out_refs[:-1]:
        xo_ref[...] = x
    ho_ref = out_refs[-1]
    ho_ref[...] = _rms_norm_rows(x, nw_ref[...]).astype(ho_ref.dtype)


def _ffn_out(u, x, w_bf, norm_w, h_dtype, emit_residual):
    n = x.shape[0]
    row = lambda width: pl.BlockSpec((ROW_BM, width), lambda i: (i, 0))
    residual = [jax.ShapeDtypeStruct((n, D_MODEL), F32)] if emit_residual else []
    out_shape = residual + [jax.ShapeDtypeStruct((n, D_MODEL), h_dtype)]
    return pl.pallas_call(
        _ffn_out_kernel,
        grid=(n // ROW_BM,),
        in_specs=[row(D_FF), row(D_MODEL),
                  _resident((D_FF, D_MODEL), lambda i: (0, 0)),
                  pl.BlockSpec((1, D_MODEL), lambda i: (0, 0))],
        out_specs=[row(D_MODEL)] * len(out_shape),
        out_shape=out_shape,
        compiler_params=_params("parallel"),
        name="ffn_out",
    )(u, x, w_bf, norm_w.reshape(1, D_MODEL))


def _rotary_tables(seq):
    half = RET_DK // 2
    inv = ROPE_BASE ** (-jnp.arange(half, dtype=F32) / half)
    ang = jnp.arange(seq, dtype=jnp.int32).astype(F32)[:, None] * inv[None, :]
    cos, sin = jnp.cos(ang), jnp.sin(ang)
    cos2, sin2 = jnp.concatenate([cos, cos], axis=1), jnp.concatenate([-sin, sin], axis=1)
    scale = jnp.asarray([RET_DK ** -0.5, 1.0], F32)[:, None, None]
    return cos2[None] * scale, sin2[None] * scale


def kernel(x, norm_mix_w, w_in, ret_gn_w, ret_proj, sgu_ln_w, sgu_ln_b, sgu_w_s, sgu_b_s,
           sgu_proj, w_out, norm_ffn_w, w_ffn_in, w_ffn_out, final_norm_w):
    batch, seq, d = x.shape
    assert d == D_MODEL and seq % IN_BM == 0 and seq % RET_TS == 0 and RET_TS % RET_T == 0
    n = batch * seq
    cos2, sin2 = _rotary_tables(seq)
    ret_consts = _retention_consts()

    xf = x.reshape(n, d)
    h = _rms_norm(xf, norm_mix_w[0], BF16)
    for l in range(DEPTH):
        zqk, zvg, zsg = _in_proj_all(h, w_in, l, cos2, sin2, seq)
        ret = _retention(zqk, zvg, ret_consts, batch, seq)
        sgu = _sgu(zsg, sgu_ln_w[l], sgu_ln_b[l], sgu_w_s[l], sgu_b_s[l])
        merged = _branch_merge(ret, sgu, zsg, ret_proj, sgu_proj, l, ret_gn_w[l])
        xf, h = _out_proj(merged, xf, w_out, l, norm_ffn_w[l])
        u, w_down_bf = _ffn_in(h, w_ffn_in, w_ffn_out, l)
        if l + 1 < DEPTH:
            xf, h = _ffn_out(u, xf, w_down_bf, norm_mix_w[l + 1], BF16, emit_residual=True)
        else:
            (out,) = _ffn_out(u, xf, w_down_bf, final_norm_w, F32, emit_residual=False)
    return out.reshape(batch, seq, d)
```

```python
import functools

import jax
import jax.numpy as jnp
import numpy as np
from jax import lax
from jax.experimental import pallas as pl
from jax.experimental.pallas import tpu as pltpu

D_MODEL = 2048
DEPTH = 4
CHUNK = 64
RET_HEADS = 8
RET_DK = 128
RET_DV = 256
RET_QK = RET_HEADS * RET_DK
RET_V = RET_HEADS * RET_DV
SGU_GROUPS = 8
SGU_LEN = 128
SGU_CH = D_MODEL // SGU_GROUPS
D_FF = 5632
ROPE_BASE = 10000.0
EPS = 1e-6

COL_QK = 0
COL_V = 2 * RET_QK
COL_G = COL_V + RET_V
COL_S = COL_G + RET_V
COL_GATE = COL_S + 2 * D_MODEL

F32 = jnp.float32
BF16 = jnp.bfloat16

V7X_VMEM_BYTES = 64 * 2**20
VMEM_LIMIT_BYTES = V7X_VMEM_BYTES - 8 * 2**20
MXU_TILE = 256

IN_BM, IN_BN = 2048, 1024
FFN_BM, FFN_BN = 2048, 512
MG_BM, MG_BN = 1024, 512
OUT_BM = 512
ROW_BM = 256
RET_T = 256
RET_TS = 2048
RET_HB = 4
SGU_BM = 1024
NORM_BM = 1024


def _params(*sem):
    return pltpu.CompilerParams(dimension_semantics=sem, vmem_limit_bytes=VMEM_LIMIT_BYTES)


def _sigmoid(x):
    return 1.0 / (1.0 + jnp.exp2(x * np.float32(-np.log2(np.e))))


def _silu(x):
    return x * _sigmoid(x)


def _gelu_x2(x):
    return x * (1.0 + lax.erf(x * np.float32(1.0 / np.sqrt(2.0))))


def _rms_norm_rows(x, w):
    return x * lax.rsqrt(jnp.mean(x * x, axis=-1, keepdims=True) + EPS) * w


def _resident(shape, index_map):
    return pl.BlockSpec(shape, index_map, pipeline_mode=pl.Buffered(1))


def _tile_slices(n, size=MXU_TILE):
    return [slice(i * size, (i + 1) * size) for i in range(n // size)]


def _norm_kernel(x_ref, w_ref, o_ref):
    o_ref[...] = _rms_norm_rows(x_ref[...], w_ref[...]).astype(o_ref.dtype)


def _rms_norm(x, w, out_dtype):
    n, d = x.shape
    return pl.pallas_call(
        _norm_kernel,
        grid=(n // NORM_BM,),
        in_specs=[pl.BlockSpec((NORM_BM, d), lambda i: (i, 0)),
                  pl.BlockSpec((1, d), lambda i: (0, 0))],
        out_specs=pl.BlockSpec((NORM_BM, d), lambda i: (i, 0)),
        out_shape=jax.ShapeDtypeStruct((n, d), out_dtype),
        compiler_params=_params("parallel"),
        name="rms_norm",
    )(x, w.reshape(1, d))


def _cast_weights_once(pairs):
    @pl.when(pl.program_id(1) == 0)
    def _():
        for w_ref, wbf_ref in pairs:
            wbf_ref[...] = w_ref[...].astype(BF16)


def _proj_units(h_ref, wbf_ref, z_ref, epilogue):
    for rows in _tile_slices(z_ref.shape[0]):
        for cols in _tile_slices(z_ref.shape[1]):
            a = jnp.dot(h_ref[rows, :], wbf_ref[:, cols], preferred_element_type=F32)
            z_ref[rows, cols] = epilogue(a, rows, cols).astype(z_ref.dtype)


def _proj_act2_kernel(h_ref, w_ref, z_ref, wbf_ref, *, acts, split):
    _cast_weights_once([(w_ref, wbf_ref)])

    @pl.when(pl.program_id(0) < split)
    def _():
        _proj_units(h_ref, wbf_ref, z_ref, lambda a, rows, cols: acts[0](a))

    @pl.when(pl.program_id(0) >= split)
    def _():
        _proj_units(h_ref, wbf_ref, z_ref, lambda a, rows, cols: acts[1](a))


def _proj_rotary_kernel(h_ref, w_ref, cos_ref, sin_ref, z_ref, wbf_ref):
    _cast_weights_once([(w_ref, wbf_ref)])
    scale = jnp.where(pl.program_id(0) == 0, np.float32(RET_DK ** -0.5), np.float32(1.0))

    def rotary(a, rows, cols):
        cos, sin = cos_ref[rows, :] * scale, sin_ref[rows, :] * scale
        heads = []
        for hh in range(MXU_TILE // RET_DK):
            t = a[:, hh * RET_DK:(hh + 1) * RET_DK]
            heads.append(t * cos + pltpu.roll(t, RET_DK // 2, axis=1) * sin)
        return jnp.concatenate(heads, axis=1)

    _proj_units(h_ref, wbf_ref, z_ref, rotary)


def _in_proj(kernel_fn, name, h, w_in, layer, col0, width, extra=(), extra_specs=()):
    n = h.shape[0]
    assert col0 % IN_BN == 0 and width % IN_BN == 0
    cb0 = col0 // IN_BN
    return pl.pallas_call(
        kernel_fn,
        grid=(width // IN_BN, n // IN_BM),
        in_specs=[
            pl.BlockSpec((IN_BM, D_MODEL), lambda j, m: (m, 0)),
            pl.BlockSpec((None, D_MODEL, IN_BN), lambda j, m: (layer, 0, cb0 + j)),
            *extra_specs,
        ],
        out_specs=pl.BlockSpec((IN_BM, IN_BN), lambda j, m: (m, j)),
        out_shape=jax.ShapeDtypeStruct((n, width), BF16),
        scratch_shapes=[pltpu.VMEM((D_MODEL, IN_BN), BF16)],
        compiler_params=_params("arbitrary", "arbitrary"),
        name=name,
    )(h, w_in, *extra)


def _in_proj_all(h, w_in, layer, cos2, sin2, seq):
    assert IN_BN == RET_QK
    pos_blocks = seq // IN_BM
    table = pl.BlockSpec((IN_BM, RET_DK), lambda j, m: (m % pos_blocks, 0))
    zqk = _in_proj(_proj_rotary_kernel, "in_proj_qk", h, w_in, layer, COL_QK, 2 * RET_QK,
                   (cos2, sin2), (table, table))
    zvg = _in_proj(functools.partial(_proj_act2_kernel, acts=(lambda a: a, _silu),
                                     split=RET_V // IN_BN),
                   "in_proj_vg", h, w_in, layer, COL_V, 2 * RET_V)
    zsg = _in_proj(functools.partial(_proj_act2_kernel, acts=(_gelu_x2, _sigmoid),
                                     split=2 * D_MODEL // IN_BN),
                   "in_proj_sg", h, w_in, layer, COL_S, 4 * D_MODEL)
    return zqk, zvg, zsg


def _retention_kernel(q_ref, k_ref, v_ref, g_ref, dmask_ref, qdec_ref, kdec_ref, cdec_ref,
                      o_ref, state_ref, acc_ref):
    @pl.when(pl.program_id(2) == 0)
    def _():
        state_ref[...] = jnp.zeros_like(state_ref)

    heads = range(RET_HB)
    qk_cols = [slice(hh * RET_DK, (hh + 1) * RET_DK) for hh in heads]
    v_cols = [slice(hh * RET_DV, (hh + 1) * RET_DV) for hh in heads]
    contract_last = (((1,), (1,)), ((), ()))
    contract_first = (((0,), (0,)), ((), ()))

    def chunk_rows(t):
        return pl.ds(pl.multiple_of(t * RET_T, RET_T), RET_T)

    def matmul_stage(t):
        rows = chunk_rows(t)
        q = [q_ref[rows, c] for c in qk_cols]
        k = [k_ref[rows, c] for c in qk_cols]
        v = [v_ref[rows, c] for c in v_cols]
        s = [lax.dot_general(q[hh], k[hh], contract_last, preferred_element_type=F32)
             for hh in heads]
        st = [state_ref[hh] for hh in heads]
        kd = [(k[hh].astype(F32) * kdec_ref[hh]).astype(BF16) for hh in heads]
        upd = [lax.dot_general(kd[hh], v[hh], contract_first, preferred_element_type=F32)
               for hh in heads]
        for hh in heads:
            state_ref[hh] = st[hh] * cdec_ref[hh] + upd[hh]
        p = [(s[hh] * dmask_ref[hh]).astype(BF16) for hh in heads]
        qs = [(q[hh].astype(F32) * qdec_ref[hh]).astype(BF16) for hh in heads]
        for hh in heads:
            acc_ref[hh] = (jnp.dot(p[hh], v[hh], preferred_element_type=F32)
                           + jnp.dot(qs[hh], st[hh].astype(BF16), preferred_element_type=F32))

    def norm_stage(t):
        rows = chunk_rows(t)
        for hh in heads:
            o = acc_ref[hh]
            mu = jnp.mean(o, axis=-1, keepdims=True)
            d = o - mu
            var = jnp.mean(d * d, axis=-1, keepdims=True)
            y = d * lax.rsqrt(var + EPS) * g_ref[rows, v_cols[hh]].astype(F32)
            o_ref[rows, v_cols[hh]] = y.astype(o_ref.dtype)

    n_chunks = RET_TS // RET_T
    matmul_stage(0)

    def body(t, carry):
        norm_stage(t - 1)
        matmul_stage(t)
        return carry

    lax.fori_loop(1, n_chunks, body, 0)
    norm_stage(n_chunks - 1)


def _retention(zqk, zvg, consts, batch, seq):
    dmask, qdec, kdec, cdec = consts
    n = zqk.shape[0]
    sb = seq // RET_TS
    hg = RET_HEADS // RET_HB
    qk_w, v_w = RET_HB * RET_DK, RET_HB * RET_DV
    head_const = lambda r, c: pl.BlockSpec((RET_HB, r, c), lambda b, h, s: (h, 0, 0))
    v_spec = pl.BlockSpec((RET_TS, v_w), lambda b, h, s: (b * sb + s, h))
    return pl.pallas_call(
        _retention_kernel,
        grid=(batch, hg, sb),
        in_specs=[
            pl.BlockSpec((RET_TS, qk_w), lambda b, h, s: (b * sb + s, h)),
            pl.BlockSpec((RET_TS, qk_w), lambda b, h, s: (b * sb + s, hg + h)),
            v_spec,
            pl.BlockSpec((RET_TS, v_w), lambda b, h, s: (b * sb + s, hg + h)),
            head_const(RET_T, RET_T),
            head_const(RET_T, RET_DK),
            head_const(RET_T, RET_DK),
            head_const(RET_DK, RET_DV),
        ],
        out_specs=v_spec,
        out_shape=jax.ShapeDtypeStruct((n, RET_V), BF16),
        scratch_shapes=[pltpu.VMEM((RET_HB, RET_DK, RET_DV), F32),
                        pltpu.VMEM((RET_HB, RET_T, RET_DV), F32)],
        compiler_params=_params("parallel", "parallel", "arbitrary"),
        name="retention",
    )(zqk, zqk, zvg, zvg, dmask, qdec, kdec, cdec)


def _retention_consts():
    log_g = jnp.log1p(-(2.0 ** (-5.0 - jnp.arange(RET_HEADS, dtype=F32))))
    idx = jnp.arange(RET_T, dtype=F32)
    dist = jnp.abs(idx[:, None] - idx[None, :])
    chunk = jnp.arange(RET_T) // CHUNK
    visible = chunk[None, :] <= chunk[:, None]
    dmask = jnp.where(visible[None], jnp.exp(log_g[:, None, None] * dist), 0.0)
    qdec = jnp.exp(log_g[:, None] * (idx[None, :] + 1.0))
    kdec = jnp.exp(log_g[:, None] * (RET_T - 1.0 - idx[None, :]))
    cdec = jnp.exp(log_g * RET_T)
    qdec = jnp.broadcast_to(qdec[:, :, None], (RET_HEADS, RET_T, RET_DK))
    kdec = jnp.broadcast_to(kdec[:, :, None], (RET_HEADS, RET_T, RET_DK))
    cdec = jnp.broadcast_to(cdec[:, None, None], (RET_HEADS, RET_DK, RET_DV))
    return dmask.astype(F32), qdec.astype(F32), kdec.astype(F32), cdec.astype(F32)


def _sgu_kernel(zu_ref, zv_ref, lnw_ref, lnb_ref, ws_ref, bs_ref, o_ref):
    zv = zv_ref[...].astype(F32)
    mu = jnp.mean(zv, axis=-1, keepdims=True)
    d = zv - mu
    var = jnp.mean(d * d, axis=-1, keepdims=True)
    vn = (d * lax.rsqrt(var + 4.0 * EPS) * lnw_ref[...] + lnb_ref[...]).astype(BF16)
    row_chunk = lax.broadcasted_iota(jnp.int32, (SGU_LEN, SGU_LEN), 0) // CHUNK
    col_chunk = lax.broadcasted_iota(jnp.int32, (SGU_LEN, SGU_LEN), 1) // CHUNK
    visible = col_chunk <= row_chunk
    for g in range(SGU_GROUPS):
        wm = jnp.where(visible, 0.5 * ws_ref[g], 0.0).astype(BF16)
        cols = slice(g * SGU_CH, (g + 1) * SGU_CH)
        for rows in _tile_slices(SGU_BM, SGU_LEN):
            mixed = jnp.dot(wm, vn[rows, cols], preferred_element_type=F32) + bs_ref[g]
            o_ref[rows, cols] = (zu_ref[rows, cols].astype(F32) * mixed).astype(o_ref.dtype)


def _sgu(zsg, ln_w, ln_b, w_s, b_s):
    n = zsg.shape[0]
    bs_b = jnp.broadcast_to(0.5 * b_s[:, :, None], (SGU_GROUPS, SGU_LEN, SGU_CH))
    return pl.pallas_call(
        _sgu_kernel,
        grid=(n // SGU_BM,),
        in_specs=[
            pl.BlockSpec((SGU_BM, D_MODEL), lambda i: (i, 0)),
            pl.BlockSpec((SGU_BM, D_MODEL), lambda i: (i, 1)),
            pl.BlockSpec((1, D_MODEL), lambda i: (0, 0)),
            pl.BlockSpec((1, D_MODEL), lambda i: (0, 0)),
            pl.BlockSpec((SGU_GROUPS, SGU_LEN, SGU_LEN), lambda i: (0, 0, 0)),
            pl.BlockSpec((SGU_GROUPS, SGU_LEN, SGU_CH), lambda i: (0, 0, 0)),
        ],
        out_specs=pl.BlockSpec((SGU_BM, D_MODEL), lambda i: (i, 0)),
        out_shape=jax.ShapeDtypeStruct((n, D_MODEL), BF16),
        compiler_params=_params("parallel"),
        name="spatial_gating",
    )(zsg, zsg, ln_w.reshape(1, D_MODEL), ln_b.reshape(1, D_MODEL), w_s, bs_b)


def _branch_merge_kernel(a_ref, b_ref, ga_ref, gb_ref, wa_ref, wb_ref, gain_ref, o_ref,
                         wabf_ref, wbbf_ref):
    @pl.when(pl.program_id(1) == 0)
    def _():
        wabf_ref[...] = (wa_ref[...] * gain_ref[...]).astype(BF16)
        wbbf_ref[...] = wb_ref[...].astype(BF16)

    for rows in _tile_slices(MG_BM):
        for cols in _tile_slices(MG_BN):
            pa = jnp.dot(a_ref[rows, :], wabf_ref[:, cols], preferred_element_type=F32)
            pb = jnp.dot(b_ref[rows, :], wbbf_ref[:, cols], preferred_element_type=F32)
            merged = ga_ref[rows, cols].astype(F32) * pa + gb_ref[rows, cols].astype(F32) * pb
            o_ref[rows, cols] = merged.astype(o_ref.dtype)


def _branch_merge(a, b, zsg, wa, wb, layer, a_gain):
    n = a.shape[0]
    nb = D_MODEL // MG_BN
    ga0 = 2 * nb
    act = pl.BlockSpec((MG_BM, D_MODEL), lambda j, m: (m, 0))
    weight = pl.BlockSpec((None, D_MODEL, MG_BN), lambda j, m: (layer, 0, j))
    return pl.pallas_call(
        _branch_merge_kernel,
        grid=(nb, n // MG_BM),
        in_specs=[act, act,
                  pl.BlockSpec((MG_BM, MG_BN), lambda j, m: (m, ga0 + j)),
                  pl.BlockSpec((MG_BM, MG_BN), lambda j, m: (m, ga0 + nb + j)),
                  weight, weight,
                  pl.BlockSpec((D_MODEL, 1), lambda j, m: (0, 0))],
        out_specs=pl.BlockSpec((MG_BM, MG_BN), lambda j, m: (m, j)),
        out_shape=jax.ShapeDtypeStruct((n, D_MODEL), BF16),
        scratch_shapes=[pltpu.VMEM((D_MODEL, MG_BN), BF16), pltpu.VMEM((D_MODEL, MG_BN), BF16)],
        compiler_params=_params("arbitrary", "arbitrary"),
        name="branch_merge",
    )(a, b, zsg, zsg, wa, wb, a_gain.reshape(RET_V, 1))


def _out_proj_kernel(m_ref, x_ref, w_ref, nw_ref, xo_ref, ho_ref, wbf_ref):
    @pl.when(pl.program_id(0) == 0)
    def _():
        wbf_ref[...] = w_ref[...].astype(BF16)

    for rows in _tile_slices(OUT_BM):
        x = x_ref[rows, :] + jnp.dot(m_ref[rows, :], wbf_ref[...], preferred_element_type=F32)
        xo_ref[rows, :] = x
        ho_ref[rows, :] = _rms_norm_rows(x, nw_ref[...]).astype(ho_ref.dtype)


def _out_proj(merged, x, w_out, layer, norm_w):
    n = x.shape[0]
    row = pl.BlockSpec((OUT_BM, D_MODEL), lambda i: (i, 0))
    return pl.pallas_call(
        _out_proj_kernel,
        grid=(n // OUT_BM,),
        in_specs=[row, row,
                  _resident((None, D_MODEL, D_MODEL), lambda i: (layer, 0, 0)),
                  pl.BlockSpec((1, D_MODEL), lambda i: (0, 0))],
        out_specs=[row, row],
        out_shape=[jax.ShapeDtypeStruct((n, D_MODEL), F32),
                   jax.ShapeDtypeStruct((n, D_MODEL), BF16)],
        scratch_shapes=[pltpu.VMEM((D_MODEL, D_MODEL), BF16)],
        compiler_params=_params("arbitrary"),
        name="out_proj",
    )(merged, x, w_out, norm_w.reshape(1, D_MODEL))


def _ffn_in_kernel(h_ref, wa_ref, wc_ref, wdown_ref, u_ref, wdown_bf_ref, wabf_ref, wcbf_ref, *,
                   row_blocks):
    _cast_weights_once([(wa_ref, wabf_ref), (wc_ref, wcbf_ref)])
    part = FFN_BN // row_blocks
    part_rows = pl.ds(pl.multiple_of(pl.program_id(1) * part, part), part)
    wdown_bf_ref[part_rows, :] = wdown_ref[part_rows, :].astype(BF16)
    for rows in _tile_slices(FFN_BM):
        for cols in _tile_slices(FFN_BN):
            a = jnp.dot(h_ref[rows, :], wabf_ref[:, cols], preferred_element_type=F32)
            g = jnp.dot(h_ref[rows, :], wcbf_ref[:, cols], preferred_element_type=F32)
            u_ref[rows, cols] = (_silu(a) * g).astype(u_ref.dtype)


def _ffn_in(h, w_ffn_in, w_ffn_out, layer):
    n = h.shape[0]
    nb = D_FF // FFN_BN
    row_blocks = n // FFN_BM
    assert FFN_BN % (16 * row_blocks) == 0
    return pl.pallas_call(
        functools.partial(_ffn_in_kernel, row_blocks=row_blocks),
        grid=(nb, row_blocks),
        in_specs=[
            pl.BlockSpec((FFN_BM, D_MODEL), lambda j, m: (m, 0)),
            pl.BlockSpec((None, D_MODEL, FFN_BN), lambda j, m: (layer, 0, j)),
            pl.BlockSpec((None, D_MODEL, FFN_BN), lambda j, m: (layer, 0, nb + j)),
            pl.BlockSpec((None, FFN_BN, D_MODEL), lambda j, m: (layer, j, 0)),
        ],
        out_specs=[pl.BlockSpec((FFN_BM, FFN_BN), lambda j, m: (m, j)),
                   pl.BlockSpec((FFN_BN, D_MODEL), lambda j, m: (j, 0))],
        out_shape=[jax.ShapeDtypeStruct((n, D_FF), BF16),
                   jax.ShapeDtypeStruct((D_FF, D_MODEL), BF16)],
        scratch_shapes=[pltpu.VMEM((D_MODEL, FFN_BN), BF16),
                        pltpu.VMEM((D_MODEL, FFN_BN), BF16)],
        compiler_params=_params("arbitrary", "arbitrary"),
        name="ffn_in",
    )(h, w_ffn_in, w_ffn_in, w_ffn_out)


def _ffn_out_kernel(u_ref, x_ref, w_ref, nw_ref, *out_refs):
    x = x_ref[...] + jnp.dot(u_ref[...], w_ref[...], preferred_element_type=F32)
    for xo_ref in out_refs[:-1]:
        xo_ref[...] = x
    ho_ref = out_refs[-1]
    ho_ref[...] = _rms_norm_rows(x, nw_ref[...]).astype(ho_ref.dtype)


def _ffn_out(u, x, w_bf, norm_w, h_dtype, emit_residual):
    n = x.shape[0]
    row = lambda width: pl.BlockSpec((ROW_BM, width), lambda i: (i, 0))
    residual = [jax.ShapeDtypeStruct((n, D_MODEL), F32)] if emit_residual else []
    out_shape = residual + [jax.ShapeDtypeStruct((n, D_MODEL), h_dtype)]
    return pl.pallas_call(
        _ffn_out_kernel,
        grid=(n // ROW_BM,),
        in_specs=[row(D_FF), row(D_MODEL),
                  _resident((D_FF, D_MODEL), lambda i: (0, 0)),
                  pl.BlockSpec((1, D_MODEL), lambda i: (0, 0))],
        out_specs=[row(D_MODEL)] * len(out_shape),
        out_shape=out_shape,
        compiler_params=_params("parallel"),
        name="ffn_out",
    )(u, x, w_bf, norm_w.reshape(1, D_MODEL))


def _rotary_tables(seq):
    half = RET_DK // 2
    inv = ROPE_BASE ** (-jnp.arange(half, dtype=F32) / half)
    ang = jnp.arange(seq, dtype=jnp.int32).astype(F32)[:, None] * inv[None, :]
    cos, sin = jnp.cos(ang), jnp.sin(ang)
    return jnp.concatenate([cos, cos], axis=1), jnp.concatenate([-sin, sin], axis=1)


def kernel(x, norm_mix_w, w_in, ret_gn_w, ret_proj, sgu_ln_w, sgu_ln_b, sgu_w_s, sgu_b_s,
           sgu_proj, w_out, norm_ffn_w, w_ffn_in, w_ffn_out, final_norm_w):
    batch, seq, d = x.shape
    assert d == D_MODEL and seq % IN_BM == 0 and seq % RET_TS == 0 and RET_TS % RET_T == 0
    n = batch * seq
    cos2, sin2 = _rotary_tables(seq)
    ret_consts = _retention_consts()

    xf = x.reshape(n, d)
    h = _rms_norm(xf, norm_mix_w[0], BF16)
    for l in range(DEPTH):
        zqk, zvg, zsg = _in_proj_all(h, w_in, l, cos2, sin2, seq)
        ret = _retention(zqk, zvg, ret_consts, batch, seq)
        sgu = _sgu(zsg, sgu_ln_w[l], sgu_ln_b[l], sgu_w_s[l], sgu_b_s[l])
        merged = _branch_merge(ret, sgu, zsg, ret_proj, sgu_proj, l, ret_gn_w[l])
        xf, h = _out_proj(merged, xf, w_out, l, norm_ffn_w[l])
        u, w_down_bf = _ffn_in(h, w_ffn_in, w_ffn_out, l)
        if l + 1 < DEPTH:
            xf, h = _ffn_out(u, xf, w_down_bf, norm_mix_w[l + 1], BF16, emit_residual=True)
        else:
            (out,) = _ffn_out(u, xf, w_down_bf, final_norm_w, F32, emit_residual=False)
    return out.reshape(batch, seq, d)
```

```python
import functools

import jax
import jax.numpy as jnp
import numpy as np
from jax import lax
from jax.experimental import pallas as pl
from jax.experimental.pallas import tpu as pltpu

D_MODEL = 2048
DEPTH = 4
CHUNK = 64
RET_HEADS = 8
RET_DK = 128
RET_DV = 256
RET_QK = RET_HEADS * RET_DK
RET_V = RET_HEADS * RET_DV
SGU_GROUPS = 8
SGU_LEN = 128
SGU_CH = D_MODEL // SGU_GROUPS
D_FF = 5632
ROPE_BASE = 10000.0
EPS = 1e-6

COL_QK = 0
COL_V = 2 * RET_QK
COL_G = COL_V + RET_V
COL_S = COL_G + RET_V
COL_GATE = COL_S + 2 * D_MODEL

F32 = jnp.float32
BF16 = jnp.bfloat16

V7X_VMEM_BYTES = 64 * 2**20
VMEM_LIMIT_BYTES = V7X_VMEM_BYTES - 8 * 2**20
MXU_TILE = 256

IN_BM, IN_BN = 2048, 1024
FFN_BM, FFN_BN = 2048, 512
MG_BM, MG_BN = 1024, 512
OUT_BM = 512
ROW_BM = 256
RET_T = 256
RET_TS = 2048
RET_HB = 4
SGU_BM = 1024
NORM_BM = 2048


def _params(*sem):
    return pltpu.CompilerParams(dimension_semantics=sem, vmem_limit_bytes=VMEM_LIMIT_BYTES)


def _sigmoid(x):
    return 1.0 / (1.0 + jnp.exp2(x * np.float32(-np.log2(np.e))))


def _silu(x):
    return x * _sigmoid(x)


def _gelu_x2(x):
    return x * (1.0 + lax.erf(x * np.float32(1.0 / np.sqrt(2.0))))


def _rms_norm_rows(x, w):
    return x * lax.rsqrt(jnp.mean(x * x, axis=-1, keepdims=True) + EPS) * w


def _resident(shape, index_map):
    return pl.BlockSpec(shape, index_map, pipeline_mode=pl.Buffered(1))


def _tile_slices(n, size=MXU_TILE):
    return [slice(i * size, (i + 1) * size) for i in range(n // size)]


def _norm_kernel(x_ref, w_ref, o_ref):
    o_ref[...] = _rms_norm_rows(x_ref[...], w_ref[...]).astype(o_ref.dtype)


def _rms_norm(x, w, out_dtype):
    n, d = x.shape
    return pl.pallas_call(
        _norm_kernel,
        grid=(n // NORM_BM,),
        in_specs=[pl.BlockSpec((NORM_BM, d), lambda i: (i, 0)),
                  pl.BlockSpec((1, d), lambda i: (0, 0))],
        out_specs=pl.BlockSpec((NORM_BM, d), lambda i: (i, 0)),
        out_shape=jax.ShapeDtypeStruct((n, d), out_dtype),
        compiler_params=_params("parallel"),
        name="rms_norm",
    )(x, w.reshape(1, d))


def _cast_weights_once(pairs):
    @pl.when(pl.program_id(1) == 0)
    def _():
        for w_ref, wbf_ref in pairs:
            wbf_ref[...] = w_ref[...].astype(BF16)


def _proj_units(h_ref, wbf_ref, z_ref, epilogue):
    for rows in _tile_slices(z_ref.shape[0]):
        for cols in _tile_slices(z_ref.shape[1]):
            a = jnp.dot(h_ref[rows, :], wbf_ref[:, cols], preferred_element_type=F32)
            z_ref[rows, cols] = epilogue(a, rows, cols).astype(z_ref.dtype)


def _proj_act2_kernel(h_ref, w_ref, z_ref, wbf_ref, *, acts, split):
    _cast_weights_once([(w_ref, wbf_ref)])

    @pl.when(pl.program_id(0) < split)
    def _():
        _proj_units(h_ref, wbf_ref, z_ref, lambda a, rows, cols: acts[0](a))

    @pl.when(pl.program_id(0) >= split)
    def _():
        _proj_units(h_ref, wbf_ref, z_ref, lambda a, rows, cols: acts[1](a))


def _proj_rotary_kernel(h_ref, w_ref, cos_ref, sin_ref, z_ref, wbf_ref):
    _cast_weights_once([(w_ref, wbf_ref)])
    scale = jnp.where(pl.program_id(0) == 0, np.float32(RET_DK ** -0.5), np.float32(1.0))

    def rotary(a, rows, cols):
        cos, sin = cos_ref[rows, :] * scale, sin_ref[rows, :] * scale
        heads = []
        for hh in range(MXU_TILE // RET_DK):
            t = a[:, hh * RET_DK:(hh + 1) * RET_DK]
            heads.append(t * cos + pltpu.roll(t, RET_DK // 2, axis=1) * sin)
        return jnp.concatenate(heads, axis=1)

    _proj_units(h_ref, wbf_ref, z_ref, rotary)


def _in_proj(kernel_fn, name, h, w_in, layer, col0, width, extra=(), extra_specs=()):
    n = h.shape[0]
    assert col0 % IN_BN == 0 and width % IN_BN == 0
    cb0 = col0 // IN_BN
    return pl.pallas_call(
        kernel_fn,
        grid=(width // IN_BN, n // IN_BM),
        in_specs=[
            pl.BlockSpec((IN_BM, D_MODEL), lambda j, m: (m, 0)),
            pl.BlockSpec((None, D_MODEL, IN_BN), lambda j, m: (layer, 0, cb0 + j)),
            *extra_specs,
        ],
        out_specs=pl.BlockSpec((IN_BM, IN_BN), lambda j, m: (m, j)),
        out_shape=jax.ShapeDtypeStruct((n, width), BF16),
        scratch_shapes=[pltpu.VMEM((D_MODEL, IN_BN), BF16)],
        compiler_params=_params("arbitrary", "arbitrary"),
        name=name,
    )(h, w_in, *extra)


def _in_proj_all(h, w_in, layer, cos2, sin2, seq):
    assert IN_BN == RET_QK
    pos_blocks = seq // IN_BM
    table = pl.BlockSpec((IN_BM, RET_DK), lambda j, m: (m % pos_blocks, 0))
    zqk = _in_proj(_proj_rotary_kernel, "in_proj_qk", h, w_in, layer, COL_QK, 2 * RET_QK,
                   (cos2, sin2), (table, table))
    zvg = _in_proj(functools.partial(_proj_act2_kernel, acts=(lambda a: a, _silu),
                                     split=RET_V // IN_BN),
                   "in_proj_vg", h, w_in, layer, COL_V, 2 * RET_V)
    zsg = _in_proj(functools.partial(_proj_act2_kernel, acts=(_gelu_x2, _sigmoid),
                                     split=2 * D_MODEL // IN_BN),
                   "in_proj_sg", h, w_in, layer, COL_S, 4 * D_MODEL)
    return zqk, zvg, zsg


def _retention_kernel(q_ref, k_ref, v_ref, g_ref, dmask_ref, qdec_ref, kdec_ref, cdec_ref,
                      o_ref, state_ref, acc_ref):
    @pl.when(pl.program_id(2) == 0)
    def _():
        state_ref[...] = jnp.zeros_like(state_ref)

    heads = range(RET_HB)
    qk_cols = [slice(hh * RET_DK, (hh + 1) * RET_DK) for hh in heads]
    v_cols = [slice(hh * RET_DV, (hh + 1) * RET_DV) for hh in heads]
    contract_last = (((1,), (1,)), ((), ()))
    contract_first = (((0,), (0,)), ((), ()))

    def chunk_rows(t):
        return pl.ds(pl.multiple_of(t * RET_T, RET_T), RET_T)

    def matmul_stage(t):
        rows = chunk_rows(t)
        q = [q_ref[rows, c] for c in qk_cols]
        k = [k_ref[rows, c] for c in qk_cols]
        v = [v_ref[rows, c] for c in v_cols]
        s = [lax.dot_general(q[hh], k[hh], contract_last, preferred_element_type=F32)
             for hh in heads]
        st = [state_ref[hh] for hh in heads]
        kd = [(k[hh].astype(F32) * kdec_ref[hh]).astype(BF16) for hh in heads]
        upd = [lax.dot_general(kd[hh], v[hh], contract_first, preferred_element_type=F32)
               for hh in heads]
        for hh in heads:
            state_ref[hh] = st[hh] * cdec_ref[hh] + upd[hh]
        p = [(s[hh] * dmask_ref[hh]).astype(BF16) for hh in heads]
        qs = [(q[hh].astype(F32) * qdec_ref[hh]).astype(BF16) for hh in heads]
        for hh in heads:
            acc_ref[hh] = (jnp.dot(p[hh], v[hh], preferred_element_type=F32)
                           + jnp.dot(qs[hh], st[hh].astype(BF16), preferred_element_type=F32))

    def norm_stage(t):
        rows = chunk_rows(t)
        for hh in heads:
            o = acc_ref[hh]
            mu = jnp.mean(o, axis=-1, keepdims=True)
            d = o - mu
            var = jnp.mean(d * d, axis=-1, keepdims=True)
            y = d * lax.rsqrt(var + EPS) * g_ref[rows, v_cols[hh]].astype(F32)
            o_ref[rows, v_cols[hh]] = y.astype(o_ref.dtype)

    n_chunks = RET_TS // RET_T
    matmul_stage(0)

    def body(t, carry):
        norm_stage(t - 1)
        matmul_stage(t)
        return carry

    lax.fori_loop(1, n_chunks, body, 0)
    norm_stage(n_chunks - 1)


def _retention(zqk, zvg, consts, batch, seq):
    dmask, qdec, kdec, cdec = consts
    n = zqk.shape[0]
    sb = seq // RET_TS
    hg = RET_HEADS // RET_HB
    qk_w, v_w = RET_HB * RET_DK, RET_HB * RET_DV
    head_const = lambda r, c: pl.BlockSpec((RET_HB, r, c), lambda b, h, s: (h, 0, 0))
    v_spec = pl.BlockSpec((RET_TS, v_w), lambda b, h, s: (b * sb + s, h))
    return pl.pallas_call(
        _retention_kernel,
        grid=(batch, hg, sb),
        in_specs=[
            pl.BlockSpec((RET_TS, qk_w), lambda b, h, s: (b * sb + s, h)),
            pl.BlockSpec((RET_TS, qk_w), lambda b, h, s: (b * sb + s, hg + h)),
            v_spec,
            pl.BlockSpec((RET_TS, v_w), lambda b, h, s: (b * sb + s, hg + h)),
            head_const(RET_T, RET_T),
            head_const(RET_T, RET_DK),
            head_const(RET_T, RET_DK),
            head_const(RET_DK, RET_DV),
        ],
        out_specs=v_spec,
        out_shape=jax.ShapeDtypeStruct((n, RET_V), BF16),
        scratch_shapes=[pltpu.VMEM((RET_HB, RET_DK, RET_DV), F32),
                        pltpu.VMEM((RET_HB, RET_T, RET_DV), F32)],
        compiler_params=_params("parallel", "parallel", "arbitrary"),
        name="retention",
    )(zqk, zqk, zvg, zvg, dmask, qdec, kdec, cdec)


def _retention_consts():
    log_g = jnp.log1p(-(2.0 ** (-5.0 - jnp.arange(RET_HEADS, dtype=F32))))
    idx = jnp.arange(RET_T, dtype=F32)
    dist = jnp.abs(idx[:, None] - idx[None, :])
    chunk = jnp.arange(RET_T) // CHUNK
    visible = chunk[None, :] <= chunk[:, None]
    dmask = jnp.where(visible[None], jnp.exp(log_g[:, None, None] * dist), 0.0)
    qdec = jnp.exp(log_g[:, None] * (idx[None, :] + 1.0))
    kdec = jnp.exp(log_g[:, None] * (RET_T - 1.0 - idx[None, :]))
    cdec = jnp.exp(log_g * RET_T)
    qdec = jnp.broadcast_to(qdec[:, :, None], (RET_HEADS, RET_T, RET_DK))
    kdec = jnp.broadcast_to(kdec[:, :, None], (RET_HEADS, RET_T, RET_DK))
    cdec = jnp.broadcast_to(cdec[:, None, None], (RET_HEADS, RET_DK, RET_DV))
    return dmask.astype(F32), qdec.astype(F32), kdec.astype(F32), cdec.astype(F32)


def _sgu_kernel(zu_ref, zv_ref, lnw_ref, lnb_ref, ws_ref, bs_ref, o_ref):
    zv = zv_ref[...].astype(F32)
    mu = jnp.mean(zv, axis=-1, keepdims=True)
    d = zv - mu
    var = jnp.mean(d * d, axis=-1, keepdims=True)
    nrm = (d * lax.rsqrt(var + 4.0 * EPS)).astype(BF16)
    row_chunk = lax.broadcasted_iota(jnp.int32, (SGU_LEN, SGU_LEN), 0) // CHUNK
    col_chunk = lax.broadcasted_iota(jnp.int32, (SGU_LEN, SGU_LEN), 1) // CHUNK
    visible = col_chunk <= row_chunk
    for g in range(SGU_GROUPS):
        wm = jnp.where(visible, 0.5 * ws_ref[g], 0.0).astype(BF16)
        cols = slice(g * SGU_CH, (g + 1) * SGU_CH)
        row_sum = jnp.sum(wm.astype(F32), axis=1, keepdims=True)
        gain = lnw_ref[:, cols]
        offset = row_sum * lnb_ref[:, cols] + bs_ref[g]
        for rows in _tile_slices(SGU_BM, SGU_LEN):
            mixed = jnp.dot(wm, nrm[rows, cols], preferred_element_type=F32) * gain + offset
            o_ref[rows, cols] = zu_ref[rows, cols] * mixed.astype(BF16)


def _sgu(zsg, ln_w, ln_b, w_s, b_s):
    n = zsg.shape[0]
    bs_b = jnp.broadcast_to(0.5 * b_s[:, :, None], (SGU_GROUPS, SGU_LEN, SGU_CH))
    return pl.pallas_call(
        _sgu_kernel,
        grid=(n // SGU_BM,),
        in_specs=[
            pl.BlockSpec((SGU_BM, D_MODEL), lambda i: (i, 0)),
            pl.BlockSpec((SGU_BM, D_MODEL), lambda i: (i, 1)),
            pl.BlockSpec((1, D_MODEL), lambda i: (0, 0)),
            pl.BlockSpec((1, D_MODEL), lambda i: (0, 0)),
            pl.BlockSpec((SGU_GROUPS, SGU_LEN, SGU_LEN), lambda i: (0, 0, 0)),
            pl.BlockSpec((SGU_GROUPS, SGU_LEN, SGU_CH), lambda i: (0, 0, 0)),
        ],
        out_specs=pl.BlockSpec((SGU_BM, D_MODEL), lambda i: (i, 0)),
        out_shape=jax.ShapeDtypeStruct((n, D_MODEL), BF16),
        compiler_params=_params("parallel"),
        name="spatial_gating",
    )(zsg, zsg, ln_w.reshape(1, D_MODEL), ln_b.reshape(1, D_MODEL), w_s, bs_b)


def _branch_merge_kernel(a_ref, b_ref, ga_ref, gb_ref, wa_ref, wb_ref, gain_ref, o_ref,
                         wabf_ref, wbbf_ref):
    @pl.when(pl.program_id(1) == 0)
    def _():
        wabf_ref[...] = (wa_ref[...] * gain_ref[...]).astype(BF16)
        wbbf_ref[...] = wb_ref[...].astype(BF16)

    for rows in _tile_slices(MG_BM):
        for cols in _tile_slices(MG_BN):
            pa = jnp.dot(a_ref[rows, :], wabf_ref[:, cols], preferred_element_type=F32)
            pb = jnp.dot(b_ref[rows, :], wbbf_ref[:, cols], preferred_element_type=F32)
            merged = ga_ref[rows, cols].astype(F32) * pa + gb_ref[rows, cols].astype(F32) * pb
            o_ref[rows, cols] = merged.astype(o_ref.dtype)


def _branch_merge(a, b, zsg, wa, wb, layer, a_gain):
    n = a.shape[0]
    nb = D_MODEL // MG_BN
    ga0 = 2 * nb
    act = pl.BlockSpec((MG_BM, D_MODEL), lambda j, m: (m, 0))
    weight = pl.BlockSpec((None, D_MODEL, MG_BN), lambda j, m: (layer, 0, j))
    return pl.pallas_call(
        _branch_merge_kernel,
        grid=(nb, n // MG_BM),
        in_specs=[act, act,
                  pl.BlockSpec((MG_BM, MG_BN), lambda j, m: (m, ga0 + j)),
                  pl.BlockSpec((MG_BM, MG_BN), lambda j, m: (m, ga0 + nb + j)),
                  weight, weight,
                  pl.BlockSpec((D_MODEL, 1), lambda j, m: (0, 0))],
        out_specs=pl.BlockSpec((MG_BM, MG_BN), lambda j, m: (m, j)),
        out_shape=jax.ShapeDtypeStruct((n, D_MODEL), BF16),
        scratch_shapes=[pltpu.VMEM((D_MODEL, MG_BN), BF16), pltpu.VMEM((D_MODEL, MG_BN), BF16)],
        compiler_params=_params("arbitrary", "arbitrary"),
        name="branch_merge",
    )(a, b, zsg, zsg, wa, wb, a_gain.reshape(RET_V, 1))


def _out_proj_kernel(m_ref, x_ref, w_ref, nw_ref, xo_ref, ho_ref, wbf_ref):
    @pl.when(pl.program_id(0) == 0)
    def _():
        wbf_ref[...] = w_ref[...].astype(BF16)

    for rows in _tile_slices(OUT_BM):
        x = x_ref[rows, :] + jnp.dot(m_ref[rows, :], wbf_ref[...], preferred_element_type=F32)
        xo_ref[rows, :] = x
        ho_ref[rows, :] = _rms_norm_rows(x, nw_ref[...]).astype(ho_ref.dtype)


def _out_proj(merged, x, w_out, layer, norm_w):
    n = x.shape[0]
    row = pl.BlockSpec((OUT_BM, D_MODEL), lambda i: (i, 0))
    return pl.pallas_call(
        _out_proj_kernel,
        grid=(n // OUT_BM,),
        in_specs=[row, row,
                  _resident((None, D_MODEL, D_MODEL), lambda i: (layer, 0, 0)),
                  pl.BlockSpec((1, D_MODEL), lambda i: (0, 0))],
        out_specs=[row, row],
        out_shape=[jax.ShapeDtypeStruct((n, D_MODEL), F32),
                   jax.ShapeDtypeStruct((n, D_MODEL), BF16)],
        scratch_shapes=[pltpu.VMEM((D_MODEL, D_MODEL), BF16)],
        compiler_params=_params("arbitrary"),
        name="out_proj",
    )(merged, x, w_out, norm_w.reshape(1, D_MODEL))


def _ffn_in_kernel(h_ref, wa_ref, wc_ref, wdown_ref, u_ref, wdown_bf_ref, wabf_ref, wcbf_ref, *,
                   row_blocks):
    _cast_weights_once([(wa_ref, wabf_ref), (wc_ref, wcbf_ref)])
    part = FFN_BN // row_blocks
    part_rows = pl.ds(pl.multiple_of(pl.program_id(1) * part, part), part)
    wdown_bf_ref[part_rows, :] = wdown_ref[part_rows, :].astype(BF16)
    for rows in _tile_slices(FFN_BM):
        for cols in _tile_slices(FFN_BN):
            a = jnp.dot(h_ref[rows, :], wabf_ref[:, cols], preferred_element_type=F32)
            g = jnp.dot(h_ref[rows, :], wcbf_ref[:, cols], preferred_element_type=F32)
            u_ref[rows, cols] = (_silu(a) * g).astype(u_ref.dtype)


def _ffn_in(h, w_ffn_in, w_ffn_out, layer):
    n = h.shape[0]
    nb = D_FF // FFN_BN
    row_blocks = n // FFN_BM
    assert FFN_BN % (16 * row_blocks) == 0
    return pl.pallas_call(
        functools.partial(_ffn_in_kernel, row_blocks=row_blocks),
        grid=(nb, row_blocks),
        in_specs=[
            pl.BlockSpec((FFN_BM, D_MODEL), lambda j, m: (m, 0)),
            pl.BlockSpec((None, D_MODEL, FFN_BN), lambda j, m: (layer, 0, j)),
            pl.BlockSpec((None, D_MODEL, FFN_BN), lambda j, m: (layer, 0, nb + j)),
            pl.BlockSpec((None, FFN_BN, D_MODEL), lambda j, m: (layer, j, 0)),
        ],
        out_specs=[pl.BlockSpec((FFN_BM, FFN_BN), lambda j, m: (m, j)),
                   pl.BlockSpec((FFN_BN, D_MODEL), lambda j, m: (j, 0))],
        out_shape=[jax.ShapeDtypeStruct((n, D_FF), BF16),
                   jax.ShapeDtypeStruct((D_FF, D_MODEL), BF16)],
        scratch_shapes=[pltpu.VMEM((D_MODEL, FFN_BN), BF16),
                        pltpu.VMEM((D_MODEL, FFN_BN), BF16)],
        compiler_params=_params("arbitrary", "arbitrary"),
        name="ffn_in",
    )(h, w_ffn_in, w_ffn_in, w_ffn_out)


def _ffn_out_kernel(u_ref, x_ref, w_ref, nw_ref, *out_refs):
    x = x_ref[...] + jnp.dot(u_ref[...], w_ref[...], preferred_element_type=F32)
    for xo_ref in out_refs[:-1]:
        xo_ref[...] = x
    ho_ref = out_refs[-1]
    ho_ref[...] = _rms_norm_rows(x, nw_ref[...]).astype(ho_ref.dtype)


def _ffn_out(u, x, w_bf, norm_w, h_dtype, emit_residual):
    n = x.shape[0]
    row = lambda width: pl.BlockSpec((ROW_BM, width), lambda i: (i, 0))
    residual = [jax.ShapeDtypeStruct((n, D_MODEL), F32)] if emit_residual else []
    out_shape = residual + [jax.ShapeDtypeStruct((n, D_MODEL), h_dtype)]
    return pl.pallas_call(
        _ffn_out_kernel,
        grid=(n // ROW_BM,),
        in_specs=[row(D_FF), row(D_MODEL),
                  _resident((D_FF, D_MODEL), lambda i: (0, 0)),
                  pl.BlockSpec((1, D_MODEL), lambda i: (0, 0))],
        out_specs=[row(D_MODEL)] * len(out_shape),
        out_shape=out_shape,
        compiler_params=_params("parallel"),
        name="ffn_out",
    )(u, x, w_bf, norm_w.reshape(1, D_MODEL))


def _rotary_tables(seq):
    half = RET_DK // 2
    inv = ROPE_BASE ** (-jnp.arange(half, dtype=F32) / half)
    ang = jnp.arange(seq, dtype=jnp.int32).astype(F32)[:, None] * inv[None, :]
    cos, sin = jnp.cos(ang), jnp.sin(ang)
    return jnp.concatenate([cos, cos], axis=1), jnp.concatenate([-sin, sin], axis=1)


def kernel(x, norm_mix_w, w_in, ret_gn_w, ret_proj, sgu_ln_w, sgu_ln_b, sgu_w_s, sgu_b_s,
           sgu_proj, w_out, norm_ffn_w, w_ffn_in, w_ffn_out, final_norm_w):
    batch, seq, d = x.shape
    assert d == D_MODEL and seq % IN_BM == 0 and seq % RET_TS == 0 and RET_TS % RET_T == 0
    n = batch * seq
    cos2, sin2 = _rotary_tables(seq)
    ret_consts = _retention_consts()

    xf = x.reshape(n, d)
    h = _rms_norm(xf, norm_mix_w[0], BF16)
    for l in range(DEPTH):
        zqk, zvg, zsg = _in_proj_all(h, w_in, l, cos2, sin2, seq)
        ret = _retention(zqk, zvg, ret_consts, batch, seq)
        sgu = _sgu(zsg, sgu_ln_w[l], sgu_ln_b[l], sgu_w_s[l], sgu_b_s[l])
        merged = _branch_merge(ret, sgu, zsg, ret_proj, sgu_proj, l, ret_gn_w[l])
        xf, h = _out_proj(merged, xf, w_out, l, norm_ffn_w[l])
        u, w_down_bf = _ffn_in(h, w_ffn_in, w_ffn_out, l)
        if l + 1 < DEPTH:
            xf, h = _ffn_out(u, xf, w_down_bf, norm_mix_w[l + 1], BF16, emit_residual=True)
        else:
            (out,) = _ffn_out(u, xf, w_down_bf, final_norm_w, F32, emit_residual=False)
    return out.reshape(batch, seq, d)
```

```python
import functools

import jax
import jax.numpy as jnp
import numpy as np
from jax import lax
from jax.experimental import pallas as pl
from jax.experimental.pallas import tpu as pltpu

D_MODEL = 2048
DEPTH = 4
CHUNK = 64
RET_HEADS = 8
RET_DK = 128
RET_DV = 256
RET_QK = RET_HEADS * RET_DK
RET_V = RET_HEADS * RET_DV
SGU_GROUPS = 8
SGU_LEN = 128
SGU_CH = D_MODEL // SGU_GROUPS
D_FF = 5632
ROPE_BASE = 10000.0
EPS = 1e-6

COL_QK = 0
COL_V = 2 * RET_QK
COL_G = COL_V + RET_V
COL_S = COL_G + RET_V
COL_GATE = COL_S + 2 * D_MODEL

F32 = jnp.float32
BF16 = jnp.bfloat16

V7X_VMEM_BYTES = 64 * 2**20
VMEM_LIMIT_BYTES = V7X_VMEM_BYTES - 8 * 2**20
MXU_TILE = 256

IN_BM, IN_BN = 2048, 1024
FFN_BM, FFN_BN = 2048, 512
MG_BM = 512
OUT_BM = 512
ROW_BM = 256
RET_T = 256
RET_TS = 2048
RET_HB = 4
SGU_BM = 1024
NORM_BM = 2048


def _params(*sem):
    return pltpu.CompilerParams(dimension_semantics=sem, vmem_limit_bytes=VMEM_LIMIT_BYTES)


def _sigmoid(x):
    return 1.0 / (1.0 + jnp.exp2(x * np.float32(-np.log2(np.e))))


def _silu(x):
    return x * _sigmoid(x)


def _gelu_x2(x):
    return x * (1.0 + lax.erf(x * np.float32(1.0 / np.sqrt(2.0))))


def _rms_norm_rows(x, w):
    return x * lax.rsqrt(jnp.mean(x * x, axis=-1, keepdims=True) + EPS) * w


def _resident(shape, index_map):
    return pl.BlockSpec(shape, index_map, pipeline_mode=pl.Buffered(1))


def _tile_slices(n, size=MXU_TILE):
    return [slice(i * size, (i + 1) * size) for i in range(n // size)]


def _norm_kernel(x_ref, w_ref, o_ref):
    o_ref[...] = _rms_norm_rows(x_ref[...], w_ref[...]).astype(o_ref.dtype)


def _rms_norm(x, w, out_dtype):
    n, d = x.shape
    return pl.pallas_call(
        _norm_kernel,
        grid=(n // NORM_BM,),
        in_specs=[pl.BlockSpec((NORM_BM, d), lambda i: (i, 0)),
                  pl.BlockSpec((1, d), lambda i: (0, 0))],
        out_specs=pl.BlockSpec((NORM_BM, d), lambda i: (i, 0)),
        out_shape=jax.ShapeDtypeStruct((n, d), out_dtype),
        compiler_params=_params("parallel"),
        name="rms_norm",
    )(x, w.reshape(1, d))


def _cast_weights_once(pairs):
    @pl.when(pl.program_id(1) == 0)
    def _():
        for w_ref, wbf_ref in pairs:
            wbf_ref[...] = w_ref[...].astype(BF16)


def _proj_units(h_ref, wbf_ref, z_ref, epilogue):
    for rows in _tile_slices(z_ref.shape[0]):
        for cols in _tile_slices(z_ref.shape[1]):
            a = jnp.dot(h_ref[rows, :], wbf_ref[:, cols], preferred_element_type=F32)
            z_ref[rows, cols] = epilogue(a, rows, cols).astype(z_ref.dtype)


def _proj_act2_kernel(h_ref, w_ref, z_ref, wbf_ref, *, acts, split):
    _cast_weights_once([(w_ref, wbf_ref)])

    @pl.when(pl.program_id(0) < split)
    def _():
        _proj_units(h_ref, wbf_ref, z_ref, lambda a, rows, cols: acts[0](a))

    @pl.when(pl.program_id(0) >= split)
    def _():
        _proj_units(h_ref, wbf_ref, z_ref, lambda a, rows, cols: acts[1](a))


def _proj_rotary_kernel(h_ref, w_ref, cos_ref, sin_ref, wa_ref, wb_ref, gain_ref,
                        z_ref, wa_bf_ref, wb_bf_ref, wbf_ref):
    _cast_weights_once([(w_ref, wbf_ref)])
    scale = jnp.where(pl.program_id(0) == 0, np.float32(RET_DK ** -0.5), np.float32(1.0))

    wa_bf_ref[...] = (wa_ref[...] * gain_ref[...]).astype(BF16)
    wb_bf_ref[...] = wb_ref[...].astype(BF16)

    def rotary(a, rows, cols):
        cos, sin = cos_ref[rows, :] * scale, sin_ref[rows, :] * scale
        heads = []
        for hh in range(MXU_TILE // RET_DK):
            t = a[:, hh * RET_DK:(hh + 1) * RET_DK]
            heads.append(t * cos + pltpu.roll(t, RET_DK // 2, axis=1) * sin)
        return jnp.concatenate(heads, axis=1)

    _proj_units(h_ref, wbf_ref, z_ref, rotary)


def _in_proj(kernel_fn, name, h, w_in, layer, col0, width, extra=(), extra_specs=(),
             extra_out_shapes=(), extra_out_specs=()):
    n = h.shape[0]
    assert col0 % IN_BN == 0 and width % IN_BN == 0
    cb0 = col0 // IN_BN
    z_spec = pl.BlockSpec((IN_BM, IN_BN), lambda j, m: (m, j))
    z_shape = jax.ShapeDtypeStruct((n, width), BF16)
    return pl.pallas_call(
        kernel_fn,
        grid=(width // IN_BN, n // IN_BM),
        in_specs=[
            pl.BlockSpec((IN_BM, D_MODEL), lambda j, m: (m, 0)),
            pl.BlockSpec((None, D_MODEL, IN_BN), lambda j, m: (layer, 0, cb0 + j)),
            *extra_specs,
        ],
        out_specs=[z_spec, *extra_out_specs] if extra_out_specs else z_spec,
        out_shape=[z_shape, *extra_out_shapes] if extra_out_shapes else z_shape,
        scratch_shapes=[pltpu.VMEM((D_MODEL, IN_BN), BF16)],
        compiler_params=_params("arbitrary", "arbitrary"),
        name=name,
    )(h, w_in, *extra)


def _in_proj_all(h, w_in, layer, cos2, sin2, seq, ret_proj, sgu_proj, ret_gain):
    assert IN_BN == RET_QK
    n = h.shape[0]
    pos_blocks = seq // IN_BM
    row_blocks = n // IN_BM
    table = pl.BlockSpec((IN_BM, RET_DK), lambda j, m: (m % pos_blocks, 0))
    steps = (2 * RET_QK // IN_BN) * row_blocks
    part = D_MODEL // steps
    assert part % 16 == 0
    step = lambda j, m: j * row_blocks + m
    w_part = pl.BlockSpec((None, part, D_MODEL), lambda j, m: (layer, step(j, m), 0))
    bf_part = pl.BlockSpec((part, D_MODEL), lambda j, m: (step(j, m), 0))
    bf_shape = jax.ShapeDtypeStruct((D_MODEL, D_MODEL), BF16)
    zqk, wa_bf, wb_bf = _in_proj(
        _proj_rotary_kernel, "in_proj_qk", h, w_in, layer, COL_QK, 2 * RET_QK,
        (cos2, sin2, ret_proj, sgu_proj, ret_gain.reshape(RET_V, 1)),
        (table, table, w_part, w_part, pl.BlockSpec((part, 1), lambda j, m: (step(j, m), 0))),
        (bf_shape, bf_shape), (bf_part, bf_part))
    zvg = _in_proj(functools.partial(_proj_act2_kernel, acts=(lambda a: a, _silu),
                                     split=RET_V // IN_BN),
                   "in_proj_vg", h, w_in, layer, COL_V, 2 * RET_V)
    zsg = _in_proj(functools.partial(_proj_act2_kernel, acts=(_gelu_x2, _sigmoid),
                                     split=2 * D_MODEL // IN_BN),
                   "in_proj_sg", h, w_in, layer, COL_S, 4 * D_MODEL)
    return zqk, zvg, zsg, wa_bf, wb_bf


def _retention_kernel(q_ref, k_ref, v_ref, g_ref, dmask_ref, qdec_ref, kdec_ref, cdec_ref,
                      o_ref, state_ref, acc_ref):
    @pl.when(pl.program_id(2) == 0)
    def _():
        state_ref[...] = jnp.zeros_like(state_ref)

    heads = range(RET_HB)
    qk_cols = [slice(hh * RET_DK, (hh + 1) * RET_DK) for hh in heads]
    v_cols = [slice(hh * RET_DV, (hh + 1) * RET_DV) for hh in heads]
    contract_last = (((1,), (1,)), ((), ()))
    contract_first = (((0,), (0,)), ((), ()))

    def chunk_rows(t):
        return pl.ds(pl.multiple_of(t * RET_T, RET_T), RET_T)

    def matmul_stage(t):
        rows = chunk_rows(t)
        q = [q_ref[rows, c] for c in qk_cols]
        k = [k_ref[rows, c] for c in qk_cols]
        v = [v_ref[rows, c] for c in v_cols]
        s = [lax.dot_general(q[hh], k[hh], contract_last, preferred_element_type=F32)
             for hh in heads]
        st = [state_ref[hh] for hh in heads]
        kd = [(k[hh].astype(F32) * kdec_ref[hh]).astype(BF16) for hh in heads]
        upd = [lax.dot_general(kd[hh], v[hh], contract_first, preferred_element_type=F32)
               for hh in heads]
        for hh in heads:
            state_ref[hh] = st[hh] * cdec_ref[hh] + upd[hh]
        p = [(s[hh] * dmask_ref[hh]).astype(BF16) for hh in heads]
        qs = [(q[hh].astype(F32) * qdec_ref[hh]).astype(BF16) for hh in heads]
        for hh in heads:
            acc_ref[hh] = (jnp.dot(p[hh], v[hh], preferred_element_type=F32)
                           + jnp.dot(qs[hh], st[hh].astype(BF16), preferred_element_type=F32))

    def norm_stage(t):
        rows = chunk_rows(t)
        for hh in heads:
            o = acc_ref[hh]
            mu = jnp.mean(o, axis=-1, keepdims=True)
            d = o - mu
            var = jnp.mean(d * d, axis=-1, keepdims=True)
            y = d * lax.rsqrt(var + EPS) * g_ref[rows, v_cols[hh]].astype(F32)
            o_ref[rows, v_cols[hh]] = y.astype(o_ref.dtype)

    n_chunks = RET_TS // RET_T
    matmul_stage(0)

    def body(t, carry):
        norm_stage(t - 1)
        matmul_stage(t)
        return carry

    lax.fori_loop(1, n_chunks, body, 0)
    norm_stage(n_chunks - 1)


def _retention(zqk, zvg, consts, batch, seq):
    dmask, qdec, kdec, cdec = consts
    n = zqk.shape[0]
    sb = seq // RET_TS
    hg = RET_HEADS // RET_HB
    qk_w, v_w = RET_HB * RET_DK, RET_HB * RET_DV
    head_const = lambda r, c: pl.BlockSpec((RET_HB, r, c), lambda b, h, s: (h, 0, 0))
    v_spec = pl.BlockSpec((RET_TS, v_w), lambda b, h, s: (b * sb + s, h))
    return pl.pallas_call(
        _retention_kernel,
        grid=(batch, hg, sb),
        in_specs=[
            pl.BlockSpec((RET_TS, qk_w), lambda b, h, s: (b * sb + s, h)),
            pl.BlockSpec((RET_TS, qk_w), lambda b, h, s: (b * sb + s, hg + h)),
            v_spec,
            pl.BlockSpec((RET_TS, v_w), lambda b, h, s: (b * sb + s, hg + h)),
            head_const(RET_T, RET_T),
            head_const(RET_T, RET_DK),
            head_const(RET_T, RET_DK),
            head_const(RET_DK, RET_DV),
        ],
        out_specs=v_spec,
        out_shape=jax.ShapeDtypeStruct((n, RET_V), BF16),
        scratch_shapes=[pltpu.VMEM((RET_HB, RET_DK, RET_DV), F32),
                        pltpu.VMEM((RET_HB, RET_T, RET_DV), F32)],
        compiler_params=_params("parallel", "parallel", "arbitrary"),
        name="retention",
    )(zqk, zqk, zvg, zvg, dmask, qdec, kdec, cdec)


def _retention_consts():
    log_g = jnp.log1p(-(2.0 ** (-5.0 - jnp.arange(RET_HEADS, dtype=F32))))
    idx = jnp.arange(RET_T, dtype=F32)
    dist = jnp.abs(idx[:, None] - idx[None, :])
    chunk = jnp.arange(RET_T) // CHUNK
    visible = chunk[None, :] <= chunk[:, None]
    dmask = jnp.where(visible[None], jnp.exp(log_g[:, None, None] * dist), 0.0)
    qdec = jnp.exp(log_g[:, None] * (idx[None, :] + 1.0))
    kdec = jnp.exp(log_g[:, None] * (RET_T - 1.0 - idx[None, :]))
    cdec = jnp.exp(log_g * RET_T)
    qdec = jnp.broadcast_to(qdec[:, :, None], (RET_HEADS, RET_T, RET_DK))
    kdec = jnp.broadcast_to(kdec[:, :, None], (RET_HEADS, RET_T, RET_DK))
    cdec = jnp.broadcast_to(cdec[:, None, None], (RET_HEADS, RET_DK, RET_DV))
    return dmask.astype(F32), qdec.astype(F32), kdec.astype(F32), cdec.astype(F32)


def _sgu_kernel(zu_ref, zv_ref, lnw_ref, lnb_ref, ws_ref, bs_ref, o_ref):
    zv = zv_ref[...].astype(F32)
    mu = jnp.mean(zv, axis=-1, keepdims=True)
    d = zv - mu
    var = jnp.mean(d * d, axis=-1, keepdims=True)
    nrm = (d * lax.rsqrt(var + 4.0 * EPS)).astype(BF16)
    row_chunk = lax.broadcasted_iota(jnp.int32, (SGU_LEN, SGU_LEN), 0) // CHUNK
    col_chunk = lax.broadcasted_iota(jnp.int32, (SGU_LEN, SGU_LEN), 1) // CHUNK
    visible = col_chunk <= row_chunk
    for g in range(SGU_GROUPS):
        wm = jnp.where(visible, 0.5 * ws_ref[g], 0.0).astype(BF16)
        cols = slice(g * SGU_CH, (g + 1) * SGU_CH)
        row_sum = jnp.sum(wm.astype(F32), axis=1, keepdims=True)
        gain = lnw_ref[:, cols]
        offset = row_sum * lnb_ref[:, cols] + bs_ref[g]
        for rows in _tile_slices(SGU_BM, SGU_LEN):
            mixed = jnp.dot(wm, nrm[rows, cols], preferred_element_type=F32) * gain + offset
            o_ref[rows, cols] = zu_ref[rows, cols] * mixed.astype(BF16)


def _sgu(zsg, ln_w, ln_b, w_s, b_s):
    n = zsg.shape[0]
    bs_b = jnp.broadcast_to(0.5 * b_s[:, :, None], (SGU_GROUPS, SGU_LEN, SGU_CH))
    return pl.pallas_call(
        _sgu_kernel,
        grid=(n // SGU_BM,),
        in_specs=[
            pl.BlockSpec((SGU_BM, D_MODEL), lambda i: (i, 0)),
            pl.BlockSpec((SGU_BM, D_MODEL), lambda i: (i, 1)),
            pl.BlockSpec((1, D_MODEL), lambda i: (0, 0)),
            pl.BlockSpec((1, D_MODEL), lambda i: (0, 0)),
            pl.BlockSpec((SGU_GROUPS, SGU_LEN, SGU_LEN), lambda i: (0, 0, 0)),
            pl.BlockSpec((SGU_GROUPS, SGU_LEN, SGU_CH), lambda i: (0, 0, 0)),
        ],
        out_specs=pl.BlockSpec((SGU_BM, D_MODEL), lambda i: (i, 0)),
        out_shape=jax.ShapeDtypeStruct((n, D_MODEL), BF16),
        compiler_params=_params("parallel"),
        name="spatial_gating",
    )(zsg, zsg, ln_w.reshape(1, D_MODEL), ln_b.reshape(1, D_MODEL), w_s, bs_b)


def _branch_merge_kernel(a_ref, b_ref, ga_ref, gb_ref, wa_ref, wb_ref, o_ref):
    for rows in _tile_slices(MG_BM):
        for cols in _tile_slices(D_MODEL):
            pa = jnp.dot(a_ref[rows, :], wa_ref[:, cols], preferred_element_type=F32)
            pb = jnp.dot(b_ref[rows, :], wb_ref[:, cols], preferred_element_type=F32)
            merged = ga_ref[rows, cols].astype(F32) * pa + gb_ref[rows, cols].astype(F32) * pb
            o_ref[rows, cols] = merged.astype(o_ref.dtype)


def _branch_merge(a, b, zsg, wa_bf, wb_bf):
    n = a.shape[0]
    gate_a = (2 * D_MODEL) // D_MODEL
    rows = lambda c: pl.BlockSpec((MG_BM, D_MODEL), lambda i: (i, c))
    weight = _resident((D_MODEL, D_MODEL), lambda i: (0, 0))
    return pl.pallas_call(
        _branch_merge_kernel,
        grid=(n // MG_BM,),
        in_specs=[rows(0), rows(0), rows(gate_a), rows(gate_a + 1), weight, weight],
        out_specs=rows(0),
        out_shape=jax.ShapeDtypeStruct((n, D_MODEL), BF16),
        compiler_params=_params("parallel"),
        name="branch_merge",
    )(a, b, zsg, zsg, wa_bf, wb_bf)


def _out_proj_kernel(m_ref, x_ref, w_ref, nw_ref, xo_ref, ho_ref, wbf_ref):
    @pl.when(pl.program_id(0) == 0)
    def _():
        wbf_ref[...] = w_ref[...].astype(BF16)

    for rows in _tile_slices(OUT_BM):
        x = x_ref[rows, :] + jnp.dot(m_ref[rows, :], wbf_ref[...], preferred_element_type=F32)
        xo_ref[rows, :] = x
        ho_ref[rows, :] = _rms_norm_rows(x, nw_ref[...]).astype(ho_ref.dtype)


def _out_proj(merged, x, w_out, layer, norm_w):
    n = x.shape[0]
    row = pl.BlockSpec((OUT_BM, D_MODEL), lambda i: (i, 0))
    return pl.pallas_call(
        _out_proj_kernel,
        grid=(n // OUT_BM,),
        in_specs=[row, row,
                  _resident((None, D_MODEL, D_MODEL), lambda i: (layer, 0, 0)),
                  pl.BlockSpec((1, D_MODEL), lambda i: (0, 0))],
        out_specs=[row, row],
        out_shape=[jax.ShapeDtypeStruct((n, D_MODEL), F32),
                   jax.ShapeDtypeStruct((n, D_MODEL), BF16)],
        scratch_shapes=[pltpu.VMEM((D_MODEL, D_MODEL), BF16)],
        compiler_params=_params("arbitrary"),
        name="out_proj",
    )(merged, x, w_out, norm_w.reshape(1, D_MODEL))


def _ffn_in_kernel(h_ref, wa_ref, wc_ref, wdown_ref, u_ref, wdown_bf_ref, wabf_ref, wcbf_ref, *,
                   row_blocks):
    _cast_weights_once([(wa_ref, wabf_ref), (wc_ref, wcbf_ref)])
    part = FFN_BN // row_blocks
    part_rows = pl.ds(pl.multiple_of(pl.program_id(1) * part, part), part)
    wdown_bf_ref[part_rows, :] = wdown_ref[part_rows, :].astype(BF16)
    for rows in _tile_slices(FFN_BM):
        for cols in _tile_slices(FFN_BN):
            a = jnp.dot(h_ref[rows, :], wabf_ref[:, cols], preferred_element_type=F32)
            g = jnp.dot(h_ref[rows, :], wcbf_ref[:, cols], preferred_element_type=F32)
            u_ref[rows, cols] = (_silu(a) * g).astype(u_ref.dtype)


def _ffn_in(h, w_ffn_in, w_ffn_out, layer):
    n = h.shape[0]
    nb = D_FF // FFN_BN
    row_blocks = n // FFN_BM
    assert FFN_BN % (16 * row_blocks) == 0
    return pl.pallas_call(
        functools.partial(_ffn_in_kernel, row_blocks=row_blocks),
        grid=(nb, row_blocks),
        in_specs=[
            pl.BlockSpec((FFN_BM, D_MODEL), lambda j, m: (m, 0)),
            pl.BlockSpec((None, D_MODEL, FFN_BN), lambda j, m: (layer, 0, j)),
            pl.BlockSpec((None, D_MODEL, FFN_BN), lambda j, m: (layer, 0, nb + j)),
            pl.BlockSpec((None, FFN_BN, D_MODEL), lambda j, m: (layer, j, 0)),
        ],
        out_specs=[pl.BlockSpec((FFN_BM, FFN_BN), lambda j, m: (m, j)),
                   pl.BlockSpec((FFN_BN, D_MODEL), lambda j, m: (j, 0))],
        out_shape=[jax.ShapeDtypeStruct((n, D_FF), BF16),
                   jax.ShapeDtypeStruct((D_FF, D_MODEL), BF16)],
        scratch_shapes=[pltpu.VMEM((D_MODEL, FFN_BN), BF16),
                        pltpu.VMEM((D_MODEL, FFN_BN), BF16)],
        compiler_params=_params("arbitrary", "arbitrary"),
        name="ffn_in",
    )(h, w_ffn_in, w_ffn_in, w_ffn_out)


def _ffn_out_kernel(u_ref, x_ref, w_ref, nw_ref, *out_refs):
    x = x_ref[...] + jnp.dot(u_ref[...], w_ref[...], preferred_element_type=F32)
    for xo_ref in out_refs[:-1]:
        xo_ref[...] = x
    ho_ref = out_refs[-1]
    ho_ref[...] = _rms_norm_rows(x, nw_ref[...]).astype(ho_ref.dtype)


def _ffn_out(u, x, w_bf, norm_w, h_dtype, emit_residual):
    n = x.shape[0]
    row = lambda width: pl.BlockSpec((ROW_BM, width), lambda i: (i, 0))
    residual = [jax.ShapeDtypeStruct((n, D_MODEL), F32)] if emit_residual else []
    out_shape = residual + [jax.ShapeDtypeStruct((n, D_MODEL), h_dtype)]
    return pl.pallas_call(
        _ffn_out_kernel,
        grid=(n // ROW_BM,),
        in_specs=[row(D_FF), row(D_MODEL),
                  _resident((D_FF, D_MODEL), lambda i: (0, 0)),
                  pl.BlockSpec((1, D_MODEL), lambda i: (0, 0))],
        out_specs=[row(D_MODEL)] * len(out_shape),
        out_shape=out_shape,
        compiler_params=_params("parallel"),
        name="ffn_out",
    )(u, x, w_bf, norm_w.reshape(1, D_MODEL))


def _rotary_tables(seq):
    half = RET_DK // 2
    inv = ROPE_BASE ** (-jnp.arange(half, dtype=F32) / half)
    ang = jnp.arange(seq, dtype=jnp.int32).astype(F32)[:, None] * inv[None, :]
    cos, sin = jnp.cos(ang), jnp.sin(ang)
    return jnp.concatenate([cos, cos], axis=1), jnp.concatenate([-sin, sin], axis=1)


def kernel(x, norm_mix_w, w_in, ret_gn_w, ret_proj, sgu_ln_w, sgu_ln_b, sgu_w_s, sgu_b_s,
           sgu_proj, w_out, norm_ffn_w, w_ffn_in, w_ffn_out, final_norm_w):
    batch, seq, d = x.shape
    assert d == D_MODEL and seq % IN_BM == 0 and seq % RET_TS == 0 and RET_TS % RET_T == 0
    n = batch * seq
    cos2, sin2 = _rotary_tables(seq)
    ret_consts = _retention_consts()

    xf = x.reshape(n, d)
    h = _rms_norm(xf, norm_mix_w[0], BF16)
    for l in range(DEPTH):
        zqk, zvg, zsg, wa_bf, wb_bf = _in_proj_all(h, w_in, l, cos2, sin2, seq,
                                                   ret_proj, sgu_proj, ret_gn_w[l])
        ret = _retention(zqk, zvg, ret_consts, batch, seq)
        sgu = _sgu(zsg, sgu_ln_w[l], sgu_ln_b[l], sgu_w_s[l], sgu_b_s[l])
        merged = _branch_merge(ret, sgu, zsg, wa_bf, wb_bf)
        xf, h = _out_proj(merged, xf, w_out, l, norm_ffn_w[l])
        u, w_down_bf = _ffn_in(h, w_ffn_in, w_ffn_out, l)
        if l + 1 < DEPTH:
            xf, h = _ffn_out(u, xf, w_down_bf, norm_mix_w[l + 1], BF16, emit_residual=True)
        else:
            (out,) = _ffn_out(u, xf, w_down_bf, final_norm_w, F32, emit_residual=False)
    return out.reshape(batch, seq, d)
```

```python
import functools

import jax
import jax.numpy as jnp
import numpy as np
from jax import lax
from jax.experimental import pallas as pl
from jax.experimental.pallas import tpu as pltpu

D_MODEL = 2048
DEPTH = 4
CHUNK = 64
RET_HEADS = 8
RET_DK = 128
RET_DV = 256
RET_QK = RET_HEADS * RET_DK
RET_V = RET_HEADS * RET_DV
SGU_GROUPS = 8
SGU_LEN = 128
SGU_CH = D_MODEL // SGU_GROUPS
D_FF = 5632
ROPE_BASE = 10000.0
EPS = 1e-6

COL_QK = 0
COL_V = 2 * RET_QK
COL_G = COL_V + RET_V
COL_S = COL_G + RET_V
COL_GATE = COL_S + 2 * D_MODEL

F32 = jnp.float32
BF16 = jnp.bfloat16

V7X_VMEM_BYTES = 64 * 2**20
VMEM_LIMIT_BYTES = V7X_VMEM_BYTES - 8 * 2**20
MXU_TILE = 256

IN_BM, IN_BN = 2048, 1024
FFN_BM, FFN_BN = 2048, 512
MG_BM = 512
OUT_BM = 512
ROW_BM = 256
RET_T = 256
RET_TS = 2048
RET_HB = 4
SGU_BM = 1024
NORM_BM = 2048


def _params(*sem):
    return pltpu.CompilerParams(dimension_semantics=sem, vmem_limit_bytes=VMEM_LIMIT_BYTES)


def _sigmoid(x):
    return 1.0 / (1.0 + jnp.exp2(x * np.float32(-np.log2(np.e))))


def _silu(x):
    return x * _sigmoid(x)


def _gelu_x2(x):
    return x * (1.0 + lax.erf(x * np.float32(1.0 / np.sqrt(2.0))))


def _rms_norm_rows(x, w):
    return x * lax.rsqrt(jnp.mean(x * x, axis=-1, keepdims=True) + EPS) * w


def _resident(shape, index_map):
    return pl.BlockSpec(shape, index_map, pipeline_mode=pl.Buffered(1))


def _tile_slices(n, size=MXU_TILE):
    return [slice(i * size, (i + 1) * size) for i in range(n // size)]


def _norm_kernel(x_ref, w_ref, o_ref):
    o_ref[...] = _rms_norm_rows(x_ref[...], w_ref[...]).astype(o_ref.dtype)


def _rms_norm(x, w, out_dtype):
    n, d = x.shape
    return pl.pallas_call(
        _norm_kernel,
        grid=(n // NORM_BM,),
        in_specs=[pl.BlockSpec((NORM_BM, d), lambda i: (i, 0)),
                  pl.BlockSpec((1, d), lambda i: (0, 0))],
        out_specs=pl.BlockSpec((NORM_BM, d), lambda i: (i, 0)),
        out_shape=jax.ShapeDtypeStruct((n, d), out_dtype),
        compiler_params=_params("parallel"),
        name="rms_norm",
    )(x, w.reshape(1, d))


def _cast_weights_once(pairs):
    @pl.when(pl.program_id(1) == 0)
    def _():
        for w_ref, wbf_ref in pairs:
            wbf_ref[...] = w_ref[...].astype(BF16)


def _proj_units(h_ref, wbf_ref, z_ref, epilogue):
    for rows in _tile_slices(z_ref.shape[0]):
        for cols in _tile_slices(z_ref.shape[1]):
            a = jnp.dot(h_ref[rows, :], wbf_ref[:, cols], preferred_element_type=F32)
            z_ref[rows, cols] = epilogue(a, rows, cols).astype(z_ref.dtype)


def _proj_act2_kernel(h_ref, w_ref, *refs, acts, split):
    if len(refs) == 4:
        side_ref, z_ref, side_bf_ref, wbf_ref = refs
        side_bf_ref[...] = side_ref[...].astype(BF16)
    else:
        z_ref, wbf_ref = refs
    _cast_weights_once([(w_ref, wbf_ref)])

    @pl.when(pl.program_id(0) < split)
    def _():
        _proj_units(h_ref, wbf_ref, z_ref, lambda a, rows, cols: acts[0](a))

    @pl.when(pl.program_id(0) >= split)
    def _():
        _proj_units(h_ref, wbf_ref, z_ref, lambda a, rows, cols: acts[1](a))


def _proj_rotary_kernel(h_ref, w_ref, cos_ref, sin_ref, wa_ref, wb_ref, gain_ref,
                        z_ref, wa_bf_ref, wb_bf_ref, wbf_ref):
    _cast_weights_once([(w_ref, wbf_ref)])
    scale = jnp.where(pl.program_id(0) == 0, np.float32(RET_DK ** -0.5), np.float32(1.0))

    wa_bf_ref[...] = (wa_ref[...] * gain_ref[...]).astype(BF16)
    wb_bf_ref[...] = wb_ref[...].astype(BF16)

    def rotary(a, rows, cols):
        cos, sin = cos_ref[rows, :] * scale, sin_ref[rows, :] * scale
        heads = []
        for hh in range(MXU_TILE // RET_DK):
            t = a[:, hh * RET_DK:(hh + 1) * RET_DK]
            heads.append(t * cos + pltpu.roll(t, RET_DK // 2, axis=1) * sin)
        return jnp.concatenate(heads, axis=1)

    _proj_units(h_ref, wbf_ref, z_ref, rotary)


def _in_proj(kernel_fn, name, h, w_in, layer, col0, width, extra=(), extra_specs=(),
             extra_out_shapes=(), extra_out_specs=()):
    n = h.shape[0]
    assert col0 % IN_BN == 0 and width % IN_BN == 0
    cb0 = col0 // IN_BN
    z_spec = pl.BlockSpec((IN_BM, IN_BN), lambda j, m: (m, j))
    z_shape = jax.ShapeDtypeStruct((n, width), BF16)
    return pl.pallas_call(
        kernel_fn,
        grid=(width // IN_BN, n // IN_BM),
        in_specs=[
            pl.BlockSpec((IN_BM, D_MODEL), lambda j, m: (m, 0)),
            pl.BlockSpec((None, D_MODEL, IN_BN), lambda j, m: (layer, 0, cb0 + j)),
            *extra_specs,
        ],
        out_specs=[z_spec, *extra_out_specs] if extra_out_specs else z_spec,
        out_shape=[z_shape, *extra_out_shapes] if extra_out_shapes else z_shape,
        scratch_shapes=[pltpu.VMEM((D_MODEL, IN_BN), BF16)],
        compiler_params=_params("arbitrary", "arbitrary"),
        name=name,
    )(h, w_in, *extra)


def _in_proj_all(h, w_in, layer, cos2, sin2, seq, ret_proj, sgu_proj, ret_gain, w_out):
    assert IN_BN == RET_QK
    n = h.shape[0]
    pos_blocks = seq // IN_BM
    row_blocks = n // IN_BM
    table = pl.BlockSpec((IN_BM, RET_DK), lambda j, m: (m % pos_blocks, 0))
    steps = (2 * RET_QK // IN_BN) * row_blocks
    part = D_MODEL // steps
    assert part % 16 == 0
    step = lambda j, m: j * row_blocks + m
    w_part = pl.BlockSpec((None, part, D_MODEL), lambda j, m: (layer, step(j, m), 0))
    bf_part = pl.BlockSpec((part, D_MODEL), lambda j, m: (step(j, m), 0))
    bf_shape = jax.ShapeDtypeStruct((D_MODEL, D_MODEL), BF16)
    zqk, wa_bf, wb_bf = _in_proj(
        _proj_rotary_kernel, "in_proj_qk", h, w_in, layer, COL_QK, 2 * RET_QK,
        (cos2, sin2, ret_proj, sgu_proj, ret_gain.reshape(RET_V, 1)),
        (table, table, w_part, w_part, pl.BlockSpec((part, 1), lambda j, m: (step(j, m), 0))),
        (bf_shape, bf_shape), (bf_part, bf_part))
    vg_steps = (2 * RET_V // IN_BN) * row_blocks
    vg_part = D_MODEL // vg_steps
    assert vg_part % 16 == 0
    zvg, wo_bf = _in_proj(
        functools.partial(_proj_act2_kernel, acts=(lambda a: a, _silu), split=RET_V // IN_BN),
        "in_proj_vg", h, w_in, layer, COL_V, 2 * RET_V,
        (w_out,), (pl.BlockSpec((None, vg_part, D_MODEL), lambda j, m: (layer, step(j, m), 0)),),
        (bf_shape,), (pl.BlockSpec((vg_part, D_MODEL), lambda j, m: (step(j, m), 0)),))
    zsg = _in_proj(functools.partial(_proj_act2_kernel, acts=(_gelu_x2, _sigmoid),
                                     split=2 * D_MODEL // IN_BN),
                   "in_proj_sg", h, w_in, layer, COL_S, 4 * D_MODEL)
    return zqk, zvg, zsg, wa_bf, wb_bf, wo_bf


def _retention_kernel(q_ref, k_ref, v_ref, g_ref, dmask_ref, qdec_ref, kdec_ref, cdec_ref,
                      o_ref, state_ref, acc_ref):
    @pl.when(pl.program_id(2) == 0)
    def _():
        state_ref[...] = jnp.zeros_like(state_ref)

    heads = range(RET_HB)
    qk_cols = [slice(hh * RET_DK, (hh + 1) * RET_DK) for hh in heads]
    v_cols = [slice(hh * RET_DV, (hh + 1) * RET_DV) for hh in heads]
    contract_last = (((1,), (1,)), ((), ()))
    contract_first = (((0,), (0,)), ((), ()))

    def chunk_rows(t):
        return pl.ds(pl.multiple_of(t * RET_T, RET_T), RET_T)

    def matmul_stage(t):
        rows = chunk_rows(t)
        q = [q_ref[rows, c] for c in qk_cols]
        k = [k_ref[rows, c] for c in qk_cols]
        v = [v_ref[rows, c] for c in v_cols]
        s = [lax.dot_general(q[hh], k[hh], contract_last, preferred_element_type=F32)
             for hh in heads]
        st = [state_ref[hh] for hh in heads]
        kd = [(k[hh].astype(F32) * kdec_ref[hh]).astype(BF16) for hh in heads]
        upd = [lax.dot_general(kd[hh], v[hh], contract_first, preferred_element_type=F32)
               for hh in heads]
        for hh in heads:
            state_ref[hh] = st[hh] * cdec_ref[hh] + upd[hh]
        p = [(s[hh] * dmask_ref[hh]).astype(BF16) for hh in heads]
        qs = [(q[hh].astype(F32) * qdec_ref[hh]).astype(BF16) for hh in heads]
        for hh in heads:
            acc_ref[hh] = (jnp.dot(p[hh], v[hh], preferred_element_type=F32)
                           + jnp.dot(qs[hh], st[hh].astype(BF16), preferred_element_type=F32))

    def norm_stage(t):
        rows = chunk_rows(t)
        for hh in heads:
            o = acc_ref[hh]
            mu = jnp.mean(o, axis=-1, keepdims=True)
            d = o - mu
            var = jnp.mean(d * d, axis=-1, keepdims=True)
            y = d * lax.rsqrt(var + EPS) * g_ref[rows, v_cols[hh]].astype(F32)
            o_ref[rows, v_cols[hh]] = y.astype(o_ref.dtype)

    n_chunks = RET_TS // RET_T
    matmul_stage(0)

    def body(t, carry):
        norm_stage(t - 1)
        matmul_stage(t)
        return carry

    lax.fori_loop(1, n_chunks, body, 0)
    norm_stage(n_chunks - 1)


def _retention(zqk, zvg, consts, batch, seq):
    dmask, qdec, kdec, cdec = consts
    n = zqk.shape[0]
    sb = seq // RET_TS
    hg = RET_HEADS // RET_HB
    qk_w, v_w = RET_HB * RET_DK, RET_HB * RET_DV
    head_const = lambda r, c: pl.BlockSpec((RET_HB, r, c), lambda b, h, s: (h, 0, 0))
    v_spec = pl.BlockSpec((RET_TS, v_w), lambda b, h, s: (b * sb + s, h))
    return pl.pallas_call(
        _retention_kernel,
        grid=(batch, hg, sb),
        in_specs=[
            pl.BlockSpec((RET_TS, qk_w), lambda b, h, s: (b * sb + s, h)),
            pl.BlockSpec((RET_TS, qk_w), lambda b, h, s: (b * sb + s, hg + h)),
            v_spec,
            pl.BlockSpec((RET_TS, v_w), lambda b, h, s: (b * sb + s, hg + h)),
            head_const(RET_T, RET_T),
            head_const(RET_T, RET_DK),
            head_const(RET_T, RET_DK),
            head_const(RET_DK, RET_DV),
        ],
        out_specs=v_spec,
        out_shape=jax.ShapeDtypeStruct((n, RET_V), BF16),
        scratch_shapes=[pltpu.VMEM((RET_HB, RET_DK, RET_DV), F32),
                        pltpu.VMEM((RET_HB, RET_T, RET_DV), F32)],
        compiler_params=_params("parallel", "parallel", "arbitrary"),
        name="retention",
    )(zqk, zqk, zvg, zvg, dmask, qdec, kdec, cdec)


def _retention_consts():
    log_g = jnp.log1p(-(2.0 ** (-5.0 - jnp.arange(RET_HEADS, dtype=F32))))
    idx = jnp.arange(RET_T, dtype=F32)
    dist = jnp.abs(idx[:, None] - idx[None, :])
    chunk = jnp.arange(RET_T) // CHUNK
    visible = chunk[None, :] <= chunk[:, None]
    dmask = jnp.where(visible[None], jnp.exp(log_g[:, None, None] * dist), 0.0)
    qdec = jnp.exp(log_g[:, None] * (idx[None, :] + 1.0))
    kdec = jnp.exp(log_g[:, None] * (RET_T - 1.0 - idx[None, :]))
    cdec = jnp.exp(log_g * RET_T)
    qdec = jnp.broadcast_to(qdec[:, :, None], (RET_HEADS, RET_T, RET_DK))
    kdec = jnp.broadcast_to(kdec[:, :, None], (RET_HEADS, RET_T, RET_DK))
    cdec = jnp.broadcast_to(cdec[:, None, None], (RET_HEADS, RET_DK, RET_DV))
    return dmask.astype(F32), qdec.astype(F32), kdec.astype(F32), cdec.astype(F32)


def _sgu_kernel(zu_ref, zv_ref, lnw_ref, lnb_ref, ws_ref, bs_ref, o_ref):
    zv = zv_ref[...].astype(F32)
    mu = jnp.mean(zv, axis=-1, keepdims=True)
    d = zv - mu
    var = jnp.mean(d * d, axis=-1, keepdims=True)
    nrm = (d * lax.rsqrt(var + 4.0 * EPS)).astype(BF16)
    row_chunk = lax.broadcasted_iota(jnp.int32, (SGU_LEN, SGU_LEN), 0) // CHUNK
    col_chunk = lax.broadcasted_iota(jnp.int32, (SGU_LEN, SGU_LEN), 1) // CHUNK
    visible = col_chunk <= row_chunk
    for g in range(SGU_GROUPS):
        wm = jnp.where(visible, 0.5 * ws_ref[g], 0.0).astype(BF16)
        cols = slice(g * SGU_CH, (g + 1) * SGU_CH)
        row_sum = jnp.sum(wm.astype(F32), axis=1, keepdims=True)
        gain = lnw_ref[:, cols]
        offset = row_sum * lnb_ref[:, cols] + bs_ref[g]
        for rows in _tile_slices(SGU_BM, SGU_LEN):
            mixed = jnp.dot(wm, nrm[rows, cols], preferred_element_type=F32) * gain + offset
            o_ref[rows, cols] = zu_ref[rows, cols] * mixed.astype(BF16)


def _sgu(zsg, ln_w, ln_b, w_s, b_s):
    n = zsg.shape[0]
    bs_b = jnp.broadcast_to(0.5 * b_s[:, :, None], (SGU_GROUPS, SGU_LEN, SGU_CH))
    return pl.pallas_call(
        _sgu_kernel,
        grid=(n // SGU_BM,),
        in_specs=[
            pl.BlockSpec((SGU_BM, D_MODEL), lambda i: (i, 0)),
            pl.BlockSpec((SGU_BM, D_MODEL), lambda i: (i, 1)),
            pl.BlockSpec((1, D_MODEL), lambda i: (0, 0)),
            pl.BlockSpec((1, D_MODEL), lambda i: (0, 0)),
            pl.BlockSpec((SGU_GROUPS, SGU_LEN, SGU_LEN), lambda i: (0, 0, 0)),
            pl.BlockSpec((SGU_GROUPS, SGU_LEN, SGU_CH), lambda i: (0, 0, 0)),
        ],
        out_specs=pl.BlockSpec((SGU_BM, D_MODEL), lambda i: (i, 0)),
        out_shape=jax.ShapeDtypeStruct((n, D_MODEL), BF16),
        compiler_params=_params("parallel"),
        name="spatial_gating",
    )(zsg, zsg, ln_w.reshape(1, D_MODEL), ln_b.reshape(1, D_MODEL), w_s, bs_b)


def _branch_merge_kernel(a_ref, b_ref, ga_ref, gb_ref, wa_ref, wb_ref, o_ref):
    for rows in _tile_slices(MG_BM):
        for cols in _tile_slices(D_MODEL):
            pa = jnp.dot(a_ref[rows, :], wa_ref[:, cols], preferred_element_type=F32)
            pb = jnp.dot(b_ref[rows, :], wb_ref[:, cols], preferred_element_type=F32)
            merged = ga_ref[rows, cols].astype(F32) * pa + gb_ref[rows, cols].astype(F32) * pb
            o_ref[rows, cols] = merged.astype(o_ref.dtype)


def _branch_merge(a, b, zsg, wa_bf, wb_bf):
    n = a.shape[0]
    gate_a = (2 * D_MODEL) // D_MODEL
    rows = lambda c: pl.BlockSpec((MG_BM, D_MODEL), lambda i: (i, c))
    weight = _resident((D_MODEL, D_MODEL), lambda i: (0, 0))
    return pl.pallas_call(
        _branch_merge_kernel,
        grid=(n // MG_BM,),
        in_specs=[rows(0), rows(0), rows(gate_a), rows(gate_a + 1), weight, weight],
        out_specs=rows(0),
        out_shape=jax.ShapeDtypeStruct((n, D_MODEL), BF16),
        compiler_params=_params("parallel"),
        name="branch_merge",
    )(a, b, zsg, zsg, wa_bf, wb_bf)


def _out_proj_kernel(m_ref, x_ref, w_ref, nw_ref, xo_ref, ho_ref):
    for rows in _tile_slices(OUT_BM):
        x = x_ref[rows, :] + jnp.dot(m_ref[rows, :], w_ref[...], preferred_element_type=F32)
        xo_ref[rows, :] = x
        ho_ref[rows, :] = _rms_norm_rows(x, nw_ref[...]).astype(ho_ref.dtype)


def _out_proj(merged, x, w_out_bf, norm_w):
    n = x.shape[0]
    row = pl.BlockSpec((OUT_BM, D_MODEL), lambda i: (i, 0))
    return pl.pallas_call(
        _out_proj_kernel,
        grid=(n // OUT_BM,),
        in_specs=[row, row,
                  _resident((D_MODEL, D_MODEL), lambda i: (0, 0)),
                  pl.BlockSpec((1, D_MODEL), lambda i: (0, 0))],
        out_specs=[row, row],
        out_shape=[jax.ShapeDtypeStruct((n, D_MODEL), F32),
                   jax.ShapeDtypeStruct((n, D_MODEL), BF16)],
        compiler_params=_params("parallel"),
        name="out_proj",
    )(merged, x, w_out_bf, norm_w.reshape(1, D_MODEL))


def _ffn_in_kernel(h_ref, wa_ref, wc_ref, wdown_ref, u_ref, wdown_bf_ref, wabf_ref, wcbf_ref, *,
                   row_blocks):
    _cast_weights_once([(wa_ref, wabf_ref), (wc_ref, wcbf_ref)])
    part = FFN_BN // row_blocks
    part_rows = pl.ds(pl.multiple_of(pl.program_id(1) * part, part), part)
    wdown_bf_ref[part_rows, :] = wdown_ref[part_rows, :].astype(BF16)
    for rows in _tile_slices(FFN_BM):
        for cols in _tile_slices(FFN_BN):
            a = jnp.dot(h_ref[rows, :], wabf_ref[:, cols], preferred_element_type=F32)
            g = jnp.dot(h_ref[rows, :], wcbf_ref[:, cols], preferred_element_type=F32)
            u_ref[rows, cols] = (_silu(a) * g).astype(u_ref.dtype)


def _ffn_in(h, w_ffn_in, w_ffn_out, layer):
    n = h.shape[0]
    nb = D_FF // FFN_BN
    row_blocks = n // FFN_BM
    assert FFN_BN % (16 * row_blocks) == 0
    return pl.pallas_call(
        functools.partial(_ffn_in_kernel, row_blocks=row_blocks),
        grid=(nb, row_blocks),
        in_specs=[
            pl.BlockSpec((FFN_BM, D_MODEL), lambda j, m: (m, 0)),
            pl.BlockSpec((None, D_MODEL, FFN_BN), lambda j, m: (layer, 0, j)),
            pl.BlockSpec((None, D_MODEL, FFN_BN), lambda j, m: (layer, 0, nb + j)),
            pl.BlockSpec((None, FFN_BN, D_MODEL), lambda j, m: (layer, j, 0)),
        ],
        out_specs=[pl.BlockSpec((FFN_BM, FFN_BN), lambda j, m: (m, j)),
                   pl.BlockSpec((FFN_BN, D_MODEL), lambda j, m: (j, 0))],
        out_shape=[jax.ShapeDtypeStruct((n, D_FF), BF16),
                   jax.ShapeDtypeStruct((D_FF, D_MODEL), BF16)],
        scratch_shapes=[pltpu.VMEM((D_MODEL, FFN_BN), BF16),
                        pltpu.VMEM((D_MODEL, FFN_BN), BF16)],
        compiler_params=_params("arbitrary", "arbitrary"),
        name="ffn_in",
    )(h, w_ffn_in, w_ffn_in, w_ffn_out)


def _ffn_out_kernel(u_ref, x_ref, w_ref, nw_ref, *out_refs):
    x = x_ref[...] + jnp.dot(u_ref[...], w_ref[...], preferred_element_type=F32)
    for xo_ref in out_refs[:-1]:
        xo_ref[...] = x
    ho_ref = out_refs[-1]
    ho_ref[...] = _rms_norm_rows(x, nw_ref[...]).astype(ho_ref.dtype)


def _ffn_out(u, x, w_bf, norm_w, h_dtype, emit_residual):
    n = x.shape[0]
    row = lambda width: pl.BlockSpec((ROW_BM, width), lambda i: (i, 0))
    residual = [jax.ShapeDtypeStruct((n, D_MODEL), F32)] if emit_residual else []
    out_shape = residual + [jax.ShapeDtypeStruct((n, D_MODEL), h_dtype)]
    return pl.pallas_call(
        _ffn_out_kernel,
        grid=(n // ROW_BM,),
        in_specs=[row(D_FF), row(D_MODEL),
                  _resident((D_FF, D_MODEL), lambda i: (0, 0)),
                  pl.BlockSpec((1, D_MODEL), lambda i: (0, 0))],
        out_specs=[row(D_MODEL)] * len(out_shape),
        out_shape=out_shape,
        compiler_params=_params("parallel"),
        name="ffn_out",
    )(u, x, w_bf, norm_w.reshape(1, D_MODEL))


def _rotary_tables(seq):
    half = RET_DK // 2
    inv = ROPE_BASE ** (-jnp.arange(half, dtype=F32) / half)
    ang = jnp.arange(seq, dtype=jnp.int32).astype(F32)[:, None] * inv[None, :]
    cos, sin = jnp.cos(ang), jnp.sin(ang)
    return jnp.concatenate([cos, cos], axis=1), jnp.concatenate([-sin, sin], axis=1)


def kernel(x, norm_mix_w, w_in, ret_gn_w, ret_proj, sgu_ln_w, sgu_ln_b, sgu_w_s, sgu_b_s,
           sgu_proj, w_out, norm_ffn_w, w_ffn_in, w_ffn_out, final_norm_w):
    batch, seq, d = x.shape
    assert d == D_MODEL and seq % IN_BM == 0 and seq % RET_TS == 0 and RET_TS % RET_T == 0
    n = batch * seq
    cos2, sin2 = _rotary_tables(seq)
    ret_consts = _retention_consts()

    xf = x.reshape(n, d)
    h = _rms_norm(xf, norm_mix_w[0], BF16)
    for l in range(DEPTH):
        zqk, zvg, zsg, wa_bf, wb_bf, wo_bf = _in_proj_all(
            h, w_in, l, cos2, sin2, seq, ret_proj, sgu_proj, ret_gn_w[l], w_out)
        ret = _retention(zqk, zvg, ret_consts, batch, seq)
        sgu = _sgu(zsg, sgu_ln_w[l], sgu_ln_b[l], sgu_w_s[l], sgu_b_s[l])
        merged = _branch_merge(ret, sgu, zsg, wa_bf, wb_bf)
        xf, h = _out_proj(merged, xf, wo_bf, norm_ffn_w[l])
        u, w_down_bf = _ffn_in(h, w_ffn_in, w_ffn_out, l)
        if l + 1 < DEPTH:
            xf, h = _ffn_out(u, xf, w_down_bf, norm_mix_w[l + 1], BF16, emit_residual=True)
        else:
            (out,) = _ffn_out(u, xf, w_down_bf, final_norm_w, F32, emit_residual=False)
    return out.reshape(batch, seq, d)
```

```python
import functools

import jax
import jax.numpy as jnp
import numpy as np
from jax import lax
from jax.experimental import pallas as pl
from jax.experimental.pallas import tpu as pltpu

D_MODEL = 2048
DEPTH = 4
CHUNK = 64
RET_HEADS = 8
RET_DK = 128
RET_DV = 256
RET_QK = RET_HEADS * RET_DK
RET_V = RET_HEADS * RET_DV
SGU_GROUPS = 8
SGU_LEN = 128
SGU_CH = D_MODEL // SGU_GROUPS
D_FF = 5632
ROPE_BASE = 10000.0
EPS = 1e-6

COL_QK = 0
COL_V = 2 * RET_QK
COL_G = COL_V + RET_V
COL_S = COL_G + RET_V
COL_GATE = COL_S + 2 * D_MODEL

F32 = jnp.float32
BF16 = jnp.bfloat16

V7X_VMEM_BYTES = 64 * 2**20
VMEM_LIMIT_BYTES = V7X_VMEM_BYTES - 8 * 2**20
MXU_TILE = 256

IN_BM, IN_BN = 2048, 1024
FFN_BM, FFN_BN = 2048, 512
MG_BM = 512
OUT_BM = 512
ROW_BM = 256
RET_T = 256
RET_TS = 2048
RET_HB = 4
SGU_BM = 1024
NORM_BM = 2048


def _params(*sem):
    return pltpu.CompilerParams(dimension_semantics=sem, vmem_limit_bytes=VMEM_LIMIT_BYTES)


def _sigmoid(x):
    return 1.0 / (1.0 + jnp.exp2(x * np.float32(-np.log2(np.e))))


def _silu(x):
    return x * _sigmoid(x)


def _gelu_x2(x):
    return x * (1.0 + lax.erf(x * np.float32(1.0 / np.sqrt(2.0))))


def _rms_norm_rows(x, w):
    return x * lax.rsqrt(jnp.mean(x * x, axis=-1, keepdims=True) + EPS) * w


def _resident(shape, index_map):
    return pl.BlockSpec(shape, index_map, pipeline_mode=pl.Buffered(1))


def _tile_slices(n, size=MXU_TILE):
    return [slice(i * size, (i + 1) * size) for i in range(n // size)]


def _norm_kernel(x_ref, w_ref, o_ref):
    o_ref[...] = _rms_norm_rows(x_ref[...], w_ref[...]).astype(o_ref.dtype)


def _rms_norm(x, w, out_dtype):
    n, d = x.shape
    return pl.pallas_call(
        _norm_kernel,
        grid=(n // NORM_BM,),
        in_specs=[pl.BlockSpec((NORM_BM, d), lambda i: (i, 0)),
                  pl.BlockSpec((1, d), lambda i: (0, 0))],
        out_specs=pl.BlockSpec((NORM_BM, d), lambda i: (i, 0)),
        out_shape=jax.ShapeDtypeStruct((n, d), out_dtype),
        compiler_params=_params("parallel"),
        name="rms_norm",
    )(x, w.reshape(1, d))


def _cast_weights_once(pairs):
    @pl.when(pl.program_id(1) == 0)
    def _():
        for w_ref, wbf_ref in pairs:
            wbf_ref[...] = w_ref[...].astype(BF16)


def _proj_units(h_ref, wbf_ref, z_ref, epilogue):
    for rows in _tile_slices(z_ref.shape[0]):
        for cols in _tile_slices(z_ref.shape[1]):
            a = jnp.dot(h_ref[rows, :], wbf_ref[:, cols], preferred_element_type=F32)
            z_ref[rows, cols] = epilogue(a, rows, cols).astype(z_ref.dtype)


def _proj_act2_kernel(h_ref, w_ref, z_ref, wbf_ref, *, acts, split):
    _cast_weights_once([(w_ref, wbf_ref)])

    @pl.when(pl.program_id(0) < split)
    def _():
        _proj_units(h_ref, wbf_ref, z_ref, lambda a, rows, cols: acts[0](a))

    @pl.when(pl.program_id(0) >= split)
    def _():
        _proj_units(h_ref, wbf_ref, z_ref, lambda a, rows, cols: acts[1](a))


def _proj_rotary_kernel(h_ref, w_ref, cos_ref, sin_ref, wa_ref, wb_ref, gain_ref,
                        z_ref, wa_bf_ref, wb_bf_ref, wbf_ref):
    _cast_weights_once([(w_ref, wbf_ref)])
    scale = jnp.where(pl.program_id(0) == 0, np.float32(RET_DK ** -0.5), np.float32(1.0))

    wa_bf_ref[...] = (wa_ref[...] * gain_ref[...]).astype(BF16)
    wb_bf_ref[...] = wb_ref[...].astype(BF16)

    def rotary(a, rows, cols):
        cos, sin = cos_ref[rows, :] * scale, sin_ref[rows, :] * scale
        heads = []
        for hh in range(MXU_TILE // RET_DK):
            t = a[:, hh * RET_DK:(hh + 1) * RET_DK]
            heads.append(t * cos + pltpu.roll(t, RET_DK // 2, axis=1) * sin)
        return jnp.concatenate(heads, axis=1)

    _proj_units(h_ref, wbf_ref, z_ref, rotary)


def _in_proj(kernel_fn, name, h, w_in, layer, col0, width, extra=(), extra_specs=(),
             extra_out_shapes=(), extra_out_specs=()):
    n = h.shape[0]
    assert col0 % IN_BN == 0 and width % IN_BN == 0
    cb0 = col0 // IN_BN
    z_spec = pl.BlockSpec((IN_BM, IN_BN), lambda j, m: (m, j))
    z_shape = jax.ShapeDtypeStruct((n, width), BF16)
    return pl.pallas_call(
        kernel_fn,
        grid=(width // IN_BN, n // IN_BM),
        in_specs=[
            pl.BlockSpec((IN_BM, D_MODEL), lambda j, m: (m, 0)),
            pl.BlockSpec((None, D_MODEL, IN_BN), lambda j, m: (layer, 0, cb0 + j)),
            *extra_specs,
        ],
        out_specs=[z_spec, *extra_out_specs] if extra_out_specs else z_spec,
        out_shape=[z_shape, *extra_out_shapes] if extra_out_shapes else z_shape,
        scratch_shapes=[pltpu.VMEM((D_MODEL, IN_BN), BF16)],
        compiler_params=_params("arbitrary", "arbitrary"),
        name=name,
    )(h, w_in, *extra)


def _in_proj_all(h, w_in, layer, cos2, sin2, seq, ret_proj, sgu_proj, ret_gain):
    assert IN_BN == RET_QK
    n = h.shape[0]
    pos_blocks = seq // IN_BM
    row_blocks = n // IN_BM
    table = pl.BlockSpec((IN_BM, RET_DK), lambda j, m: (m % pos_blocks, 0))
    steps = (2 * RET_QK // IN_BN) * row_blocks
    part = D_MODEL // steps
    assert part % 16 == 0
    step = lambda j, m: j * row_blocks + m
    w_part = pl.BlockSpec((None, part, D_MODEL), lambda j, m: (layer, step(j, m), 0))
    bf_part = pl.BlockSpec((part, D_MODEL), lambda j, m: (step(j, m), 0))
    bf_shape = jax.ShapeDtypeStruct((D_MODEL, D_MODEL), BF16)
    zqk, wa_bf, wb_bf = _in_proj(
        _proj_rotary_kernel, "in_proj_qk", h, w_in, layer, COL_QK, 2 * RET_QK,
        (cos2, sin2, ret_proj, sgu_proj, ret_gain.reshape(RET_V, 1)),
        (table, table, w_part, w_part, pl.BlockSpec((part, 1), lambda j, m: (step(j, m), 0))),
        (bf_shape, bf_shape), (bf_part, bf_part))
    zvg = _in_proj(functools.partial(_proj_act2_kernel, acts=(lambda a: a, _silu),
                                     split=RET_V // IN_BN),
                   "in_proj_vg", h, w_in, layer, COL_V, 2 * RET_V)
    zsg = _in_proj(functools.partial(_proj_act2_kernel, acts=(_gelu_x2, _sigmoid),
                                     split=2 * D_MODEL // IN_BN),
                   "in_proj_sg", h, w_in, layer, COL_S, 4 * D_MODEL)
    return zqk, zvg, zsg, wa_bf, wb_bf


def _retention_kernel(q_ref, k_ref, v_ref, g_ref, dmask_ref, qdec_ref, kdec_ref, cdec_ref,
                      o_ref, state_ref, acc_ref):
    @pl.when(pl.program_id(2) == 0)
    def _():
        state_ref[...] = jnp.zeros_like(state_ref)

    heads = range(RET_HB)
    qk_cols = [slice(hh * RET_DK, (hh + 1) * RET_DK) for hh in heads]
    v_cols = [slice(hh * RET_DV, (hh + 1) * RET_DV) for hh in heads]
    contract_last = (((1,), (1,)), ((), ()))
    contract_first = (((0,), (0,)), ((), ()))

    def chunk_rows(t):
        return pl.ds(pl.multiple_of(t * RET_T, RET_T), RET_T)

    def matmul_stage(t):
        rows = chunk_rows(t)
        q = [q_ref[rows, c] for c in qk_cols]
        k = [k_ref[rows, c] for c in qk_cols]
        v = [v_ref[rows, c] for c in v_cols]
        s = [lax.dot_general(q[hh], k[hh], contract_last, preferred_element_type=F32)
             for hh in heads]
        st = [state_ref[hh] for hh in heads]
        kd = [(k[hh].astype(F32) * kdec_ref[hh]).astype(BF16) for hh in heads]
        upd = [lax.dot_general(kd[hh], v[hh], contract_first, preferred_element_type=F32)
               for hh in heads]
        for hh in heads:
            state_ref[hh] = st[hh] * cdec_ref[hh] + upd[hh]
        p = [(s[hh] * dmask_ref[hh]).astype(BF16) for hh in heads]
        qs = [(q[hh].astype(F32) * qdec_ref[hh]).astype(BF16) for hh in heads]
        for hh in heads:
            acc_ref[hh] = (jnp.dot(p[hh], v[hh], preferred_element_type=F32)
                           + jnp.dot(qs[hh], st[hh].astype(BF16), preferred_element_type=F32))

    def norm_stage(t):
        rows = chunk_rows(t)
        for hh in heads:
            o = acc_ref[hh]
            mu = jnp.mean(o, axis=-1, keepdims=True)
            d = o - mu
            var = jnp.mean(d * d, axis=-1, keepdims=True)
            y = d * lax.rsqrt(var + EPS) * g_ref[rows, v_cols[hh]].astype(F32)
            o_ref[rows, v_cols[hh]] = y.astype(o_ref.dtype)

    n_chunks = RET_TS // RET_T
    matmul_stage(0)

    def body(t, carry):
        norm_stage(t - 1)
        matmul_stage(t)
        return carry

    lax.fori_loop(1, n_chunks, body, 0)
    norm_stage(n_chunks - 1)


def _retention(zqk, zvg, consts, batch, seq):
    dmask, qdec, kdec, cdec = consts
    n = zqk.shape[0]
    sb = seq // RET_TS
    hg = RET_HEADS // RET_HB
    qk_w, v_w = RET_HB * RET_DK, RET_HB * RET_DV
    head_const = lambda r, c: pl.BlockSpec((RET_HB, r, c), lambda b, h, s: (h, 0, 0))
    v_spec = pl.BlockSpec((RET_TS, v_w), lambda b, h, s: (b * sb + s, h))
    return pl.pallas_call(
        _retention_kernel,
        grid=(batch, hg, sb),
        in_specs=[
            pl.BlockSpec((RET_TS, qk_w), lambda b, h, s: (b * sb + s, h)),
            pl.BlockSpec((RET_TS, qk_w), lambda b, h, s: (b * sb + s, hg + h)),
            v_spec,
            pl.BlockSpec((RET_TS, v_w), lambda b, h, s: (b * sb + s, hg + h)),
            head_const(RET_T, RET_T),
            head_const(RET_T, RET_DK),
            head_const(RET_T, RET_DK),
            head_const(RET_DK, RET_DV),
        ],
        out_specs=v_spec,
        out_shape=jax.ShapeDtypeStruct((n, RET_V), BF16),
        scratch_shapes=[pltpu.VMEM((RET_HB, RET_DK, RET_DV), F32),
                        pltpu.VMEM((RET_HB, RET_T, RET_DV), F32)],
        compiler_params=_params("parallel", "parallel", "arbitrary"),
        name="retention",
    )(zqk, zqk, zvg, zvg, dmask, qdec, kdec, cdec)


def _retention_consts():
    log_g = jnp.log1p(-(2.0 ** (-5.0 - jnp.arange(RET_HEADS, dtype=F32))))
    idx = jnp.arange(RET_T, dtype=F32)
    dist = jnp.abs(idx[:, None] - idx[None, :])
    chunk = jnp.arange(RET_T) // CHUNK
    visible = chunk[None, :] <= chunk[:, None]
    dmask = jnp.where(visible[None], jnp.exp(log_g[:, None, None] * dist), 0.0)
    qdec = jnp.exp(log_g[:, None] * (idx[None, :] + 1.0))
    kdec = jnp.exp(log_g[:, None] * (RET_T - 1.0 - idx[None, :]))
    cdec = jnp.exp(log_g * RET_T)
    qdec = jnp.broadcast_to(qdec[:, :, None], (RET_HEADS, RET_T, RET_DK))
    kdec = jnp.broadcast_to(kdec[:, :, None], (RET_HEADS, RET_T, RET_DK))
    cdec = jnp.broadcast_to(cdec[:, None, None], (RET_HEADS, RET_DK, RET_DV))
    return dmask.astype(F32), qdec.astype(F32), kdec.astype(F32), cdec.astype(F32)


def _sgu_kernel(zu_ref, zv_ref, lnw_ref, lnb_ref, ws_ref, bs_ref, o_ref):
    zv = zv_ref[...].astype(F32)
    mu = jnp.mean(zv, axis=-1, keepdims=True)
    d = zv - mu
    var = jnp.mean(d * d, axis=-1, keepdims=True)
    nrm = (d * lax.rsqrt(var + 4.0 * EPS)).astype(BF16)
    row_chunk = lax.broadcasted_iota(jnp.int32, (SGU_LEN, SGU_LEN), 0) // CHUNK
    col_chunk = lax.broadcasted_iota(jnp.int32, (SGU_LEN, SGU_LEN), 1) // CHUNK
    visible = col_chunk <= row_chunk
    for g in range(SGU_GROUPS):
        wm = jnp.where(visible, 0.5 * ws_ref[g], 0.0).astype(BF16)
        cols = slice(g * SGU_CH, (g + 1) * SGU_CH)
        row_sum = jnp.sum(wm.astype(F32), axis=1, keepdims=True)
        gain = lnw_ref[:, cols]
        offset = row_sum * lnb_ref[:, cols] + bs_ref[g]
        for rows in _tile_slices(SGU_BM, SGU_LEN):
            mixed = jnp.dot(wm, nrm[rows, cols], preferred_element_type=F32) * gain + offset
            o_ref[rows, cols] = zu_ref[rows, cols] * mixed.astype(BF16)


def _sgu(zsg, ln_w, ln_b, w_s, b_s):
    n = zsg.shape[0]
    bs_b = jnp.broadcast_to(0.5 * b_s[:, :, None], (SGU_GROUPS, SGU_LEN, SGU_CH))

    def outer(zsg_ref, lnw_ref, lnb_ref, ws_ref, bs_ref, o_ref):
        def step(zu_ref, zv_ref, out_ref):
            _sgu_kernel(zu_ref, zv_ref, lnw_ref, lnb_ref, ws_ref, bs_ref, out_ref)

        tile = lambda c: pl.BlockSpec((SGU_BM, D_MODEL), lambda i: (i, c),
                                      pipeline_mode=pl.Buffered(3))
        pltpu.emit_pipeline(
            step,
            grid=(n // SGU_BM,),
            in_specs=[tile(0), tile(1)],
            out_specs=[pl.BlockSpec((SGU_BM, D_MODEL), lambda i: (i, 0))],
        )(zsg_ref, zsg_ref, o_ref)

    whole = pl.BlockSpec(memory_space=pltpu.VMEM)
    return pl.pallas_call(
        outer,
        in_specs=[pl.BlockSpec(memory_space=pl.ANY), whole, whole, whole, whole],
        out_specs=pl.BlockSpec(memory_space=pl.ANY),
        out_shape=jax.ShapeDtypeStruct((n, D_MODEL), BF16),
        compiler_params=pltpu.CompilerParams(vmem_limit_bytes=VMEM_LIMIT_BYTES),
        name="spatial_gating",
    )(zsg, ln_w.reshape(1, D_MODEL), ln_b.reshape(1, D_MODEL), w_s, bs_b)


def _branch_merge_kernel(a_ref, b_ref, ga_ref, gb_ref, wa_ref, wb_ref, o_ref):
    for rows in _tile_slices(MG_BM):
        for cols in _tile_slices(D_MODEL):
            pa = jnp.dot(a_ref[rows, :], wa_ref[:, cols], preferred_element_type=F32)
            pb = jnp.dot(b_ref[rows, :], wb_ref[:, cols], preferred_element_type=F32)
            merged = ga_ref[rows, cols].astype(F32) * pa + gb_ref[rows, cols].astype(F32) * pb
            o_ref[rows, cols] = merged.astype(o_ref.dtype)


def _branch_merge(a, b, zsg, wa_bf, wb_bf):
    n = a.shape[0]
    gate_a = (2 * D_MODEL) // D_MODEL
    rows = lambda c: pl.BlockSpec((MG_BM, D_MODEL), lambda i: (i, c))
    weight = _resident((D_MODEL, D_MODEL), lambda i: (0, 0))
    return pl.pallas_call(
        _branch_merge_kernel,
        grid=(n // MG_BM,),
        in_specs=[rows(0), rows(0), rows(gate_a), rows(gate_a + 1), weight, weight],
        out_specs=rows(0),
        out_shape=jax.ShapeDtypeStruct((n, D_MODEL), BF16),
        compiler_params=_params("parallel"),
        name="branch_merge",
    )(a, b, zsg, zsg, wa_bf, wb_bf)


def _out_proj_kernel(m_ref, x_ref, w_ref, nw_ref, xo_ref, ho_ref, wbf_ref):
    @pl.when(pl.program_id(0) == 0)
    def _():
        wbf_ref[...] = w_ref[...].astype(BF16)

    for rows in _tile_slices(OUT_BM):
        x = x_ref[rows, :] + jnp.dot(m_ref[rows, :], wbf_ref[...], preferred_element_type=F32)
        xo_ref[rows, :] = x
        ho_ref[rows, :] = _rms_norm_rows(x, nw_ref[...]).astype(ho_ref.dtype)


def _out_proj(merged, x, w_out, layer, norm_w):
    n = x.shape[0]
    row = pl.BlockSpec((OUT_BM, D_MODEL), lambda i: (i, 0))
    return pl.pallas_call(
        _out_proj_kernel,
        grid=(n // OUT_BM,),
        in_specs=[row, row,
                  _resident((None, D_MODEL, D_MODEL), lambda i: (layer, 0, 0)),
                  pl.BlockSpec((1, D_MODEL), lambda i: (0, 0))],
        out_specs=[row, row],
        out_shape=[jax.ShapeDtypeStruct((n, D_MODEL), F32),
                   jax.ShapeDtypeStruct((n, D_MODEL), BF16)],
        scratch_shapes=[pltpu.VMEM((D_MODEL, D_MODEL), BF16)],
        compiler_params=_params("arbitrary"),
        name="out_proj",
    )(merged, x, w_out, norm_w.reshape(1, D_MODEL))


def _ffn_in_kernel(h_ref, wa_ref, wc_ref, wdown_ref, u_ref, wdown_bf_ref, wabf_ref, wcbf_ref, *,
                   row_blocks):
    _cast_weights_once([(wa_ref, wabf_ref), (wc_ref, wcbf_ref)])
    part = FFN_BN // row_blocks
    part_rows = pl.ds(pl.multiple_of(pl.program_id(1) * part, part), part)
    wdown_bf_ref[part_rows, :] = wdown_ref[part_rows, :].astype(BF16)
    for rows in _tile_slices(FFN_BM):
        for cols in _tile_slices(FFN_BN):
            a = jnp.dot(h_ref[rows, :], wabf_ref[:, cols], preferred_element_type=F32)
            g = jnp.dot(h_ref[rows, :], wcbf_ref[:, cols], preferred_element_type=F32)
            u_ref[rows, cols] = (_silu(a) * g).astype(u_ref.dtype)


def _ffn_in(h, w_ffn_in, w_ffn_out, layer):
    n = h.shape[0]
    nb = D_FF // FFN_BN
    row_blocks = n // FFN_BM
    assert FFN_BN % (16 * row_blocks) == 0
    return pl.pallas_call(
        functools.partial(_ffn_in_kernel, row_blocks=row_blocks),
        grid=(nb, row_blocks),
        in_specs=[
            pl.BlockSpec((FFN_BM, D_MODEL), lambda j, m: (m, 0)),
            pl.BlockSpec((None, D_MODEL, FFN_BN), lambda j, m: (layer, 0, j)),
            pl.BlockSpec((None, D_MODEL, FFN_BN), lambda j, m: (layer, 0, nb + j)),
            pl.BlockSpec((None, FFN_BN, D_MODEL), lambda j, m: (layer, j, 0)),
        ],
        out_specs=[pl.BlockSpec((FFN_BM, FFN_BN), lambda j, m: (m, j)),
                   pl.BlockSpec((FFN_BN, D_MODEL), lambda j, m: (j, 0))],
        out_shape=[jax.ShapeDtypeStruct((n, D_FF), BF16),
                   jax.ShapeDtypeStruct((D_FF, D_MODEL), BF16)],
        scratch_shapes=[pltpu.VMEM((D_MODEL, FFN_BN), BF16),
                        pltpu.VMEM((D_MODEL, FFN_BN), BF16)],
        compiler_params=_params("arbitrary", "arbitrary"),
        name="ffn_in",
    )(h, w_ffn_in, w_ffn_in, w_ffn_out)


def _ffn_out_kernel(u_ref, x_ref, w_ref, nw_ref, *out_refs):
    x = x_ref[...] + jnp.dot(u_ref[...], w_ref[...], preferred_element_type=F32)
    for xo_ref in out_refs[:-1]:
        xo_ref[...] = x
    ho_ref = out_refs[-1]
    ho_ref[...] = _rms_norm_rows(x, nw_ref[...]).astype(ho_ref.dtype)


def _ffn_out(u, x, w_bf, norm_w, h_dtype, emit_residual):
    n = x.shape[0]
    row = lambda width: pl.BlockSpec((ROW_BM, width), lambda i: (i, 0))
    residual = [jax.ShapeDtypeStruct((n, D_MODEL), F32)] if emit_residual else []
    out_shape = residual + [jax.ShapeDtypeStruct((n, D_MODEL), h_dtype)]
    return pl.pallas_call(
        _ffn_out_kernel,
        grid=(n // ROW_BM,),
        in_specs=[row(D_FF), row(D_MODEL),
                  _resident((D_FF, D_MODEL), lambda i: (0, 0)),
                  pl.BlockSpec((1, D_MODEL), lambda i: (0, 0))],
        out_specs=[row(D_MODEL)] * len(out_shape),
        out_shape=out_shape,
        compiler_params=_params("parallel"),
        name="ffn_out",
    )(u, x, w_bf, norm_w.reshape(1, D_MODEL))


def _rotary_tables(seq):
    half = RET_DK // 2
    inv = ROPE_BASE ** (-jnp.arange(half, dtype=F32) / half)
    ang = jnp.arange(seq, dtype=jnp.int32).astype(F32)[:, None] * inv[None, :]
    cos, sin = jnp.cos(ang), jnp.sin(ang)
    return jnp.concatenate([cos, cos], axis=1), jnp.concatenate([-sin, sin], axis=1)


def kernel(x, norm_mix_w, w_in, ret_gn_w, ret_proj, sgu_ln_w, sgu_ln_b, sgu_w_s, sgu_b_s,
           sgu_proj, w_out, norm_ffn_w, w_ffn_in, w_ffn_out, final_norm_w):
    batch, seq, d = x.shape
    assert d == D_MODEL and seq % IN_BM == 0 and seq % RET_TS == 0 and RET_TS % RET_T == 0
    n = batch * seq
    cos2, sin2 = _rotary_tables(seq)
    ret_consts = _retention_consts()

    xf = x.reshape(n, d)
    h = _rms_norm(xf, norm_mix_w[0], BF16)
    for l in range(DEPTH):
        zqk, zvg, zsg, wa_bf, wb_bf = _in_proj_all(h, w_in, l, cos2, sin2, seq,
                                                   ret_proj, sgu_proj, ret_gn_w[l])
        ret = _retention(zqk, zvg, ret_consts, batch, seq)
        sgu = _sgu(zsg, sgu_ln_w[l], sgu_ln_b[l], sgu_w_s[l], sgu_b_s[l])
        merged = _branch_merge(ret, sgu, zsg, wa_bf, wb_bf)
        xf, h = _out_proj(merged, xf, w_out, l, norm_ffn_w[l])
        u, w_down_bf = _ffn_in(h, w_ffn_in, w_ffn_out, l)
        if l + 1 < DEPTH:
            xf, h = _ffn_out(u, xf, w_down_bf, norm_mix_w[l + 1], BF16, emit_residual=True)
        else:
            (out,) = _ffn_out(u, xf, w_down_bf, final_norm_w, F32, emit_residual=False)
    return out.reshape(batch, seq, d)
```
